```python
import math, functools
import jax, jax.numpy as jnp
from jax import lax
import numpy as np


D_MODEL = 1024
BATCH = 8
SEQ = 4096
DEPTH = 2
DEC_BATCH = 32
DEC_SEQ = 1
PAST_LEN = 16384
PAGE_SIZE = 128

W_A = D_MODEL // 4
W_B = D_MODEL // 4
W_C = D_MODEL // 2
A_BLOCKS = 4
A_BLOCK = W_A // A_BLOCKS
CONV_W = 4
LRU_C = 8.0
B_HEADS = 4
B_VDIM = W_B // B_HEADS
B_KDIM = B_VDIM
B_CHUNK = 64
C_HEADS = 4
C_V = W_C // C_HEADS
C_QK = C_V // 2
Q_BLOCK = 128
N_GROUPS = 4
EXP_PER_GROUP = 4
N_EXPERTS = N_GROUPS * EXP_PER_GROUP
TOP_K = 2
D_EXPERT = D_MODEL // 4
IN_SPLITS = (W_A, W_A,
             B_HEADS * B_KDIM, B_HEADS * B_KDIM, W_B, W_B,
             C_HEADS * 2 * C_QK, C_HEADS * 2 * C_QK, W_C)
IN_WIDTH = sum(IN_SPLITS)
ALPHA = (2 * DEPTH) ** 0.25
BETA = (8 * DEPTH) ** -0.25
LN_EPS = 1e-5
RMS_EPS = 1e-6

kernel_name = 'hymba_hybrid_rglru_hgrn2_diffattn_hmoe_step'

F32 = jnp.float32


def layer_norm(x, g, b):
    xf = x.astype(F32)
    mu = jnp.mean(xf, -1, keepdims=True)
    var = jnp.mean(jnp.square(xf - mu), -1, keepdims=True)
    return ((xf - mu) * lax.rsqrt(var + LN_EPS) * g + b).astype(x.dtype)


def rms_norm(x, g):
    xf = x.astype(F32)
    return xf * lax.rsqrt(jnp.mean(xf * xf, -1, keepdims=True) + RMS_EPS) * g


def alibi_slopes(n):
    return jnp.asarray([2.0 ** (-8.0 * (h + 1) / n) for h in range(n)], dtype=F32)


def causal_conv(x, buf, w, b):
    L = x.shape[1]
    xp = jnp.concatenate([buf.astype(x.dtype), x], axis=1)
    y = b + xp[:, 0:L] * w[0]
    for j in range(1, CONV_W):
        y = y + xp[:, j:j + L] * w[j]
    return y, xp[:, L:]


def _linear_combine(left, right):
    a_l, b_l = left
    a_r, b_r = right
    return a_l * a_r, a_r * b_l + b_r


def rg_lru(x, h0, wr, br, wi, bi, lam):
    bsz, L, _ = x.shape
    xb = x.reshape(bsz, L, A_BLOCKS, A_BLOCK)
    r = jax.nn.sigmoid((jnp.einsum('blnc,ncd->blnd', xb, wr).reshape(bsz, L, W_A) + br).astype(F32))
    i = jax.nn.sigmoid((jnp.einsum('blnc,ncd->blnd', xb, wi).reshape(bsz, L, W_A) + bi).astype(F32))
    log_a = LRU_C * r * jax.nn.log_sigmoid(lam.astype(F32))
    a = jnp.exp(log_a)
    u = jnp.sqrt(-jnp.expm1(2.0 * log_a)) * (i * x.astype(F32))
    a_cum, h = lax.associative_scan(_linear_combine, (a, u), axis=1)
    h = h + a_cum * h0.astype(F32)[:, None]
    return h.astype(x.dtype), h[:, -1]


def _hgrn_chunk(S, inp):
    q, k, v, lf = inp
    C = q.shape[1]
    b = jnp.cumsum(lf, axis=1)
    tri = jnp.tril(jnp.ones((C, C), dtype=bool))[None, :, :, None, None]
    diff = b[:, :, None] - b[:, None, :]
    decay = jnp.exp(jnp.where(tri, diff, -jnp.inf))
    att = jnp.einsum('bthk,bshk,btshk->bhts', q, k, decay)
    o = jnp.einsum('bhts,bshv->bthv', att, v) + jnp.einsum('bthk,bhkv->bthv', q * jnp.exp(b), S)
    b_last = b[:, -1]
    S_new = jnp.exp(b_last)[..., None] * S + jnp.einsum('bshk,bshv->bhkv', k * jnp.exp(b_last[:, None] - b), v)
    return S_new, o


def hgrn2(q_raw, f_raw, i_raw, g_raw, S0, lb, norm_g):
    bsz, L, _ = q_raw.shape
    split = lambda t, d: t.reshape(bsz, L, B_HEADS, d).astype(F32)
    q = jax.nn.silu(split(q_raw, B_KDIM))
    fr = split(f_raw, B_KDIM)
    lb = lb.reshape(B_HEADS, B_KDIM)
    log_f = jnp.logaddexp(jnp.log(lb), jnp.log1p(-lb) + jax.nn.log_sigmoid(fr))
    k = (1.0 - lb) * jax.nn.sigmoid(-fr)
    v = split(i_raw, B_VDIM)
    C = B_CHUNK if L % B_CHUNK == 0 else L
    nc = L // C
    to_chunks = lambda t: jnp.moveaxis(t.reshape(bsz, nc, C, B_HEADS, t.shape[-1]), 1, 0)
    S, o = lax.scan(_hgrn_chunk, S0.astype(F32), (to_chunks(q), to_chunks(k), to_chunks(v), to_chunks(log_f)))
    o = jnp.moveaxis(o, 0, 1).reshape(bsz, L, B_HEADS, B_VDIM)
    o = rms_norm(o, norm_g) * jax.nn.silu(split(g_raw, B_VDIM))
    return o.reshape(bsz, L, W_B).astype(q_raw.dtype), S


def diff_softmax_attend(q, k, v, qpos, kpos, lam, slopes):
    bsz, lk = k.shape[:2]
    k = k.reshape(bsz, lk, C_HEADS, 2, C_QK)
    s = jnp.einsum('bqhcd,bkhcd->bhcqk', q, k).astype(F32) * (C_QK ** -0.5)
    rel = (qpos[:, None] - kpos[None, :]).astype(F32)
    s = s - slopes[:, None, None, None] * rel
    s = jnp.where(kpos[None, :] <= qpos[:, None], s, -jnp.inf)
    p = jax.nn.softmax(s, axis=-1)
    w = p[:, :, 0] - lam * p[:, :, 1]
    return jnp.einsum('bhqk,bkhv->bqhv', w.astype(v.dtype), v)


def attend_prompt(q, k, v, lam, slopes):
    bsz, L = q.shape[:2]
    nb = L // Q_BLOCK
    qb = jnp.moveaxis(q.reshape(bsz, nb, Q_BLOCK, C_HEADS, 2, C_QK), 1, 0)
    kpos = jnp.arange(L)

    def one_block(args):
        q_blk, start = args
        return diff_softmax_attend(q_blk, k, v, start + jnp.arange(Q_BLOCK), kpos, lam, slopes)

    o = lax.map(one_block, (qb, jnp.arange(nb) * Q_BLOCK))
    return jnp.moveaxis(o, 0, 1).reshape(bsz, L, C_HEADS, C_V)


def attend_paged(q, k_new, v_new, lam, slopes, cache_k, cache_v, layer, page_table):
    nb, L = q.shape[:2]
    k_past = cache_k[layer, page_table].reshape(nb, -1, C_HEADS, 2 * C_QK)
    v_past = cache_v[layer, page_table].reshape(nb, -1, C_HEADS, C_V)
    past = k_past.shape[1]
    k_all = jnp.concatenate([k_past, k_new.astype(k_past.dtype)], axis=1)
    v_all = jnp.concatenate([v_past, v_new.astype(v_past.dtype)], axis=1)
    qpos = past + jnp.arange(L)
    kpos = jnp.arange(past + L)
    return diff_softmax_attend(q, k_all, v_all, qpos, kpos, lam, slopes)


def diff_attention(q_raw, k_raw, v_raw, attend, lam, lam_init, norm_g):
    bsz, L, _ = q_raw.shape
    q = q_raw.reshape(bsz, L, C_HEADS, 2, C_QK)
    k_rows = k_raw.reshape(bsz, L, C_HEADS, 2 * C_QK)
    v_rows = v_raw.reshape(bsz, L, C_HEADS, C_V)
    o = attend(q, k_rows, v_rows, lam)
    o = rms_norm(o, norm_g) * (1.0 - lam_init)
    return o.reshape(bsz, L, W_C).astype(q_raw.dtype), k_rows, v_rows


def hier_moe(x, wg, bg, we, be, w_gate, w_up, w_down):
    shape = x.shape
    xt = x.reshape(-1, shape[-1])
    pg = jax.nn.softmax((xt @ wg + bg).astype(F32), axis=-1)
    p_top, g_idx = lax.top_k(pg, 1)
    le = (xt @ we + be).astype(F32).reshape(-1, N_GROUPS, EXP_PER_GROUP)
    le = jnp.take_along_axis(le, g_idx[:, :, None], axis=1)[:, 0]
    top_v, top_i = lax.top_k(le, TOP_K)
    gate = p_top * jax.nn.softmax(top_v, axis=-1)
    eid = g_idx * EXP_PER_GROUP + top_i
    combine = jnp.einsum('tk,tke->te', gate, jax.nn.one_hot(eid, N_EXPERTS, dtype=F32))
    h = jax.nn.silu(jnp.einsum('td,edf->tef', xt, w_gate)) * jnp.einsum('td,edf->tef', xt, w_up)
    y = jnp.einsum('tef,te,efd->td', h, combine.astype(h.dtype), w_down)
    return y.reshape(shape)


def decoder_layer(x, conv_buf, h0, S0, attend, lb, lam, lam_init,
                  w_in, conv_w, conv_b, rg_wr, rg_br, rg_wi, rg_bi, rg_lambda, hg_norm_g, dl_norm_g,
                  w_out, ln1_g, ln1_b, ln2_g, ln2_b, moe_wg, moe_bg, moe_we, moe_be,
                  ex_w_gate, ex_w_up, ex_w_down):
    z = x @ w_in
    xa, ga, qb, fb, ib, gb, qc, kc, vc = jnp.split(z, np.cumsum(IN_SPLITS)[:-1].tolist(), axis=-1)
    xa_c, conv_new = causal_conv(xa, conv_buf, conv_w, conv_b)
    ha, h_new = rg_lru(xa_c, h0, rg_wr, rg_br, rg_wi, rg_bi, rg_lambda)
    ya = ha * jax.nn.gelu(ga)
    yb, S_new = hgrn2(qb, fb, ib, gb, S0, lb, hg_norm_g)
    yc, k_rows, v_rows = diff_attention(qc, kc, vc, attend, lam, lam_init, dl_norm_g)
    mixed = jnp.concatenate([ya, yb, yc], axis=-1) @ w_out
    x = layer_norm(ALPHA * x + mixed, ln1_g, ln1_b)
    x = layer_norm(ALPHA * x + hier_moe(x, moe_wg, moe_bg, moe_we, moe_be, ex_w_gate, ex_w_up, ex_w_down), ln2_g, ln2_b)
    return x, conv_new, h_new, S_new, k_rows, v_rows


def setup_inputs(seed: int = 0) -> dict:
    key = jax.random.key(seed)
    ks = iter(jax.random.split(key, 48))
    nrm = lambda shape, s=1.0: s * jax.random.normal(next(ks), shape, dtype=F32)
    n_pages = PAST_LEN // PAGE_SIZE
    n_used = DEC_BATCH * n_pages
    n_pool = n_used + max(1, n_used // 4)
    page_table = jax.random.permutation(next(ks), n_pool)[:n_used].reshape(DEC_BATCH, n_pages).astype(jnp.int32)
    u = jax.random.uniform(next(ks), (DEPTH, W_A), dtype=F32, minval=0.9, maxval=0.999)
    sa = u ** (1.0 / LRU_C)
    rg_lambda = jnp.log(sa) - jnp.log1p(-sa)
    return {
        'x_prompt': nrm((BATCH, SEQ, D_MODEL)),
        'x_sample': nrm((DEC_BATCH, DEC_SEQ, D_MODEL)),
        'cache_k': nrm((DEPTH, n_pool, PAGE_SIZE, C_HEADS, 2 * C_QK)),
        'cache_v': nrm((DEPTH, n_pool, PAGE_SIZE, C_HEADS, C_V)),
        'state_conv': nrm((DEPTH, DEC_BATCH, CONV_W - 1, W_A)),
        'state_rglru': nrm((DEPTH, DEC_BATCH, W_A), 0.5),
        'state_hgrn': nrm((DEPTH, DEC_BATCH, B_HEADS, B_KDIM, B_VDIM), 0.3),
        'page_table': page_table,
        'emb_ln_g': 1.0 + nrm((D_MODEL,), 0.02),
        'emb_ln_b': nrm((D_MODEL,), 0.02),
        'w_in': nrm((DEPTH, D_MODEL, IN_WIDTH), D_MODEL ** -0.5),
        'conv_w': nrm((DEPTH, CONV_W, W_A), CONV_W ** -0.5),
        'conv_b': nrm((DEPTH, W_A), 0.02),
        'rg_wr': nrm((DEPTH, A_BLOCKS, A_BLOCK, A_BLOCK), A_BLOCK ** -0.5),
        'rg_br': nrm((DEPTH, W_A), 0.02),
        'rg_wi': nrm((DEPTH, A_BLOCKS, A_BLOCK, A_BLOCK), A_BLOCK ** -0.5),
        'rg_bi': nrm((DEPTH, W_A), 0.02),
        'rg_lambda': rg_lambda,
        'hg_lower': nrm((DEPTH, B_HEADS * B_KDIM)),
        'hg_norm_g': 1.0 + nrm((DEPTH, B_VDIM), 0.02),
        'dl_lq1': nrm((DEPTH, C_QK), 0.1),
        'dl_lk1': nrm((DEPTH, C_QK), 0.1),
        'dl_lq2': nrm((DEPTH, C_QK), 0.1),
        'dl_lk2': nrm((DEPTH, C_QK), 0.1),
        'dl_norm_g': 1.0 + nrm((DEPTH, C_V), 0.02),
        'w_out': nrm((DEPTH, D_MODEL, D_MODEL), BETA * D_MODEL ** -0.5),
        'ln1_g': 1.0 + nrm((DEPTH, D_MODEL), 0.02),
        'ln1_b': nrm((DEPTH, D_MODEL), 0.02),
        'ln2_g': 1.0 + nrm((DEPTH, D_MODEL), 0.02),
        'ln2_b': nrm((DEPTH, D_MODEL), 0.02),
        'moe_wg': nrm((DEPTH, D_MODEL, N_GROUPS), D_MODEL ** -0.5),
        'moe_bg': nrm((DEPTH, N_GROUPS), 0.01),
        'moe_we': nrm((DEPTH, D_MODEL, N_EXPERTS), D_MODEL ** -0.5),
        'moe_be': nrm((DEPTH, N_EXPERTS), 0.01),
        'ex_w_gate': nrm((DEPTH, N_EXPERTS, D_MODEL, D_EXPERT), D_MODEL ** -0.5),
        'ex_w_up': nrm((DEPTH, N_EXPERTS, D_MODEL, D_EXPERT), D_MODEL ** -0.5),
        'ex_w_down': nrm((DEPTH, N_EXPERTS, D_EXPERT, D_MODEL), BETA * D_EXPERT ** -0.5),
    }


def reference(x_prompt, x_sample, cache_k, cache_v, state_conv, state_rglru, state_hgrn, page_table,
              emb_ln_g, emb_ln_b, w_in, conv_w, conv_b, rg_wr, rg_br, rg_wi, rg_bi, rg_lambda,
              hg_lower, hg_norm_g, dl_lq1, dl_lk1, dl_lq2, dl_lk2, dl_norm_g, w_out,
              ln1_g, ln1_b, ln2_g, ln2_b, moe_wg, moe_bg, moe_we, moe_be,
              ex_w_gate, ex_w_up, ex_w_down):
    lb_all = jnp.cumsum(jax.nn.softmax(hg_lower.astype(F32), axis=0), axis=0)
    lower_bounds = lb_all - lb_all[0]
    slopes = alibi_slopes(C_HEADS)
    xp = layer_norm(x_prompt, emb_ln_g, emb_ln_b)
    xs = layer_norm(x_sample, emb_ln_g, emb_ln_b)
    bp = x_prompt.shape[0]
    conv_p, conv_s, lru_p, lru_s, hg_p, hg_s, k_p, k_s, v_p, v_s = ([] for _ in range(10))
    for l in range(DEPTH):
        lam_init = 0.8 - 0.6 * math.exp(-0.3 * l)
        lam = (jnp.exp(jnp.sum(dl_lq1[l] * dl_lk1[l])) - jnp.exp(jnp.sum(dl_lq2[l] * dl_lk2[l])) + lam_init).astype(F32)
        lp = (w_in[l], conv_w[l], conv_b[l], rg_wr[l], rg_br[l], rg_wi[l], rg_bi[l], rg_lambda[l],
              hg_norm_g[l], dl_norm_g[l], w_out[l], ln1_g[l], ln1_b[l], ln2_g[l], ln2_b[l],
              moe_wg[l], moe_bg[l], moe_we[l], moe_be[l], ex_w_gate[l], ex_w_up[l], ex_w_down[l])
        att_p = functools.partial(attend_prompt, slopes=slopes)
        att_s = functools.partial(attend_paged, slopes=slopes, cache_k=cache_k, cache_v=cache_v,
                                  layer=l, page_table=page_table)
        xp, c1, h1, s1, k1, v1 = decoder_layer(
            xp, jnp.zeros((bp, CONV_W - 1, W_A), xp.dtype), jnp.zeros((bp, W_A), xp.dtype),
            jnp.zeros((bp, B_HEADS, B_KDIM, B_VDIM), F32), att_p, lower_bounds[l], lam, lam_init, *lp)
        xs, c2, h2, s2, k2, v2 = decoder_layer(
            xs, state_conv[l], state_rglru[l], state_hgrn[l], att_s, lower_bounds[l], lam, lam_init, *lp)
        conv_p.append(c1); conv_s.append(c2); lru_p.append(h1); lru_s.append(h2)
        hg_p.append(s1); hg_s.append(s2); k_p.append(k1); k_s.append(k2); v_p.append(v1); v_s.append(v2)
    return (xp, xs,
            jnp.stack(conv_p), jnp.stack(conv_s),
            jnp.stack(lru_p), jnp.stack(lru_s),
            jnp.stack(hg_p), jnp.stack(hg_s),
            jnp.stack(k_p), jnp.stack(k_s),
            jnp.stack(v_p), jnp.stack(v_s))
```

```python
import functools
import math

import jax
import jax.numpy as jnp
from jax import lax
from jax.experimental import pallas as pl
from jax.experimental.pallas import tpu as pltpu

F32 = jnp.float32
BF16 = jnp.bfloat16

A_BLOCKS = 4
CONV_W = 4
LRU_C = 8.0
B_HEADS = 4
C_HEADS = 4
N_GROUPS = 4
EXP_PER_GROUP = 4
N_EXPERTS = N_GROUPS * EXP_PER_GROUP
LN_EPS = 1e-5
RMS_EPS = 1e-6
NEG_INF = float("-inf")

VMEM_LIMIT_BYTES = 56 * 1024 * 1024
HIGHEST = lax.Precision.HIGHEST


def _cparams(*sem):
    return pltpu.CompilerParams(dimension_semantics=sem, vmem_limit_bytes=VMEM_LIMIT_BYTES)


def _row_tile(n, want):
    return want if n % want == 0 else n


def _layer_norm(x, g, b):
    mu = jnp.mean(x, axis=-1, keepdims=True)
    xc = x - mu
    var = jnp.mean(xc * xc, axis=-1, keepdims=True)
    return xc * lax.rsqrt(var + LN_EPS) * g + b


def _sigmoid(x):
    return 1.0 / (1.0 + jnp.exp(-x))


def _silu(x):
    return x * _sigmoid(x)


def _log_sigmoid(x):
    return jnp.minimum(x, 0.0) - jnp.log1p(jnp.exp(-jnp.abs(x)))


def _gelu_tanh(x):
    c = math.sqrt(2.0 / math.pi)
    return 0.5 * x * (1.0 + jnp.tanh(c * (x + 0.044715 * (x * x * x))))


def _dot(a, b, precision=None):
    return jnp.dot(a, b, preferred_element_type=F32, precision=precision)


def _dot_nt(a, b, precision=None):
    return lax.dot_general(a, b, (((1,), (1,)), ((), ())), preferred_element_type=F32,
                           precision=precision)


def _dot_tn(a, b, precision=None):
    return lax.dot_general(a, b, (((0,), (0,)), ((), ())), preferred_element_type=F32,
                           precision=precision)


def _shift_rows(x, d, fill):
    row = lax.broadcasted_iota(jnp.int32, x.shape, 0)
    return jnp.where(row >= d, pltpu.roll(x, d, 0), fill)


def _head_block_mask(n, head):
    r = lax.broadcasted_iota(jnp.int32, (n, n), 0) // head
    c = lax.broadcasted_iota(jnp.int32, (n, n), 1) // head
    return r == c


def _proj_in_body(x_ref, g_ref, b_ref, w_ref, *out_refs, apply_ln, splits):
    x = x_ref[...]
    if apply_ln:
        x = _layer_norm(x, g_ref[...], b_ref[...])
        out_refs[0][...] = x
        out_refs = out_refs[1:]
    xb = x.astype(BF16)
    off = 0
    for o_ref, width in zip(out_refs, splits):
        o_ref[...] = _dot(xb, w_ref[:, off:off + width])
        off += width


def _proj_in(x, g, b, w_bf16, splits, apply_ln):
    t, d = x.shape
    tm = _row_tile(t, 512)
    n = w_bf16.shape[1]
    out_shape = [jax.ShapeDtypeStruct((t, s), F32) for s in splits]
    out_specs = [pl.BlockSpec((tm, s), lambda i: (i, 0)) for s in splits]
    if apply_ln:
        out_shape = [jax.ShapeDtypeStruct((t, d), F32)] + out_shape
        out_specs = [pl.BlockSpec((tm, d), lambda i: (i, 0))] + out_specs
    return pl.pallas_call(
        functools.partial(_proj_in_body, apply_ln=apply_ln, splits=splits),
        grid=(t // tm,),
        in_specs=[pl.BlockSpec((tm, d), lambda i: (i, 0)),
                  pl.BlockSpec((1, d), lambda i: (0, 0)),
                  pl.BlockSpec((1, d), lambda i: (0, 0)),
                  pl.BlockSpec((d, n), lambda i: (0, 0))],
        out_specs=out_specs,
        out_shape=out_shape,
        compiler_params=_cparams("parallel"),
        name="proj_in",
    )(x, g, b, w_bf16)


def _rglru_gates(y, wg_ref, bg_ref, lam_ref, wa):
    gm = _dot(y.astype(BF16), wg_ref[...]) + bg_ref[...]
    r = _sigmoid(gm[:, :wa])
    i = _sigmoid(gm[:, wa:])
    log_a = LRU_C * r * _log_sigmoid(lam_ref[...])
    a = jnp.exp(log_a)
    th = jnp.tanh(log_a)
    u = jnp.sqrt(-2.0 * th / (1.0 - th)) * (i * y)
    return a, u


def _mixer_a_prompt_body(za_ref, cs_ref, h0_ref, cw_ref, cb_ref, wg_ref, bg_ref, lam_ref,
                         ya_ref, cn_ref, hn_ref, xbuf, *, tt, wa):
    t = pl.program_id(1)

    @pl.when(t == 0)
    def _():
        xbuf[5:8, :] = cs_ref[0]
        hn_ref[0] = h0_ref[0]

    x = za_ref[0, :, :wa]
    ga = za_ref[0, :, wa:]
    xbuf[8:8 + tt, :] = x
    cw = cw_ref[...]
    y = (cb_ref[...] + cw[0:1] * xbuf[5:5 + tt, :] + cw[1:2] * xbuf[6:6 + tt, :]
         + cw[2:3] * xbuf[7:7 + tt, :] + cw[3:4] * x)
    tail = x[tt - (CONV_W - 1):tt]
    xbuf[5:8, :] = tail
    cn_ref[0] = tail

    a, u = _rglru_gates(y, wg_ref, bg_ref, lam_ref, wa)
    d = 1
    while d < tt:
        a_s = _shift_rows(a, d, 1.0)
        u_s = _shift_rows(u, d, 0.0)
        u = a * u_s + u
        a = a * a_s
        d *= 2
    h = u + a * hn_ref[0]
    hn_ref[0] = h[tt - 1:tt]
    ya_ref[0] = h * _gelu_tanh(ga)


def _mixer_a_prompt(za, conv_state, h0, cw, cb, wg_bf16, bg, lam):
    bsz, length, two_wa = za.shape
    wa = two_wa // 2
    tt = _row_tile(length, 256)
    const = lambda shape: pl.BlockSpec(shape, lambda b, t: (0,) * len(shape))
    return pl.pallas_call(
        functools.partial(_mixer_a_prompt_body, tt=tt, wa=wa),
        grid=(bsz, length // tt),
        in_specs=[pl.BlockSpec((1, tt, two_wa), lambda b, t: (b, t, 0)),
                  pl.BlockSpec((1, CONV_W - 1, wa), lambda b, t: (b, 0, 0)),
                  pl.BlockSpec((1, 1, wa), lambda b, t: (b, 0, 0)),
                  const((CONV_W, wa)), const((1, wa)), const((wa, two_wa)), const((1, two_wa)),
                  const((1, wa))],
        out_specs=[pl.BlockSpec((1, tt, wa), lambda b, t: (b, t, 0)),
                   pl.BlockSpec((1, CONV_W - 1, wa), lambda b, t: (b, 0, 0)),
                   pl.BlockSpec((1, 1, wa), lambda b, t: (b, 0, 0))],
        out_shape=[jax.ShapeDtypeStruct((bsz, length, wa), F32),
                   jax.ShapeDtypeStruct((bsz, CONV_W - 1, wa), F32),
                   jax.ShapeDtypeStruct((bsz, 1, wa), F32)],
        scratch_shapes=[pltpu.VMEM((8 + tt, wa), F32)],
        compiler_params=_cparams("parallel", "arbitrary"),
        name="mixer_a_prompt",
    )(za, conv_state, h0.reshape(bsz, 1, wa), cw, cb, wg_bf16, bg, lam)


def _mixer_a_step_body(za_ref, c0_ref, c1_ref, c2_ref, h0_ref, cw_ref, cb_ref, wg_ref, bg_ref,
                       lam_ref, ya_ref, hn_ref, *, wa):
    x = za_ref[:, :wa]
    ga = za_ref[:, wa:]
    cw = cw_ref[...]
    y = (cb_ref[...] + cw[0:1] * c0_ref[...] + cw[1:2] * c1_ref[...] + cw[2:3] * c2_ref[...]
         + cw[3:4] * x)
    a, u = _rglru_gates(y, wg_ref, bg_ref, lam_ref, wa)
    h = u + a * h0_ref[...]
    hn_ref[...] = h
    ya_ref[...] = h * _gelu_tanh(ga)


def _mixer_a_step(za, conv_state, h0, cw, cb, wg_bf16, bg, lam):
    n, two_wa = za.shape
    wa = two_wa // 2
    ya, hn = pl.pallas_call(
        functools.partial(_mixer_a_step_body, wa=wa),
        out_shape=[jax.ShapeDtypeStruct((n, wa), F32), jax.ShapeDtypeStruct((n, wa), F32)],
        compiler_params=pltpu.CompilerParams(vmem_limit_bytes=VMEM_LIMIT_BYTES),
        name="mixer_a_step",
    )(za, conv_state[:, 0], conv_state[:, 1], conv_state[:, 2], h0, cw, cb, wg_bf16, bg, lam)
    conv_new = jnp.concatenate([conv_state[:, 1:], za[:, None, :wa]], axis=1)
    return ya, conv_new, hn


def _hgrn_lower_bound(hl_ref, layer):
    rows = [hl_ref[i:i + 1, :] for i in range(hl_ref.shape[0])]
    mx = functools.reduce(jnp.maximum, rows)
    es = [jnp.exp(r - mx) for r in rows]
    tot = functools.reduce(lambda p, q: p + q, es)
    ps = [e / tot for e in es]
    cum = ps[0]
    first = cum
    for i in range(1, layer + 1):
        cum = cum + ps[i]
    return cum - first


def _hgrn_inputs(z, lb, hk):
    q = _silu(z[:, 0:hk])
    fr = z[:, hk:2 * hk]
    v = z[:, 2 * hk:3 * hk]
    g = z[:, 3 * hk:4 * hk]
    log_lb = jnp.log(lb)
    c = jnp.log1p(-lb) + _log_sigmoid(fr)
    log_f = jnp.maximum(log_lb, c) + jnp.log1p(jnp.exp(-jnp.abs(log_lb - c)))
    k = (1.0 - lb) * _sigmoid(-fr)
    return q, log_f, k, v, g


def _hgrn_finish(o, g, ng_ref, bd_f32, head):
    ms = _dot(o * o, bd_f32, HIGHEST) * (1.0 / head)
    return o * lax.rsqrt(ms + RMS_EPS) * ng_ref[...] * _silu(g)


def _mixer_b_prompt_body(zb_ref, st0_ref, hl_ref, ng_ref, yb_ref, st_ref, *, layer, rows, hk, head):
    c = pl.program_id(1)

    @pl.when(c == 0)
    def _():
        st_ref[0] = st0_ref[0]

    lb = _hgrn_lower_bound(hl_ref, layer)
    q, lf, k, v, g = _hgrn_inputs(zb_ref[0], lb, hk)

    row = lax.broadcasted_iota(jnp.int32, (rows, hk), 0)
    b = lf
    d = 1
    while d < rows:
        b = b + _shift_rows(b, d, 0.0)
        d *= 2

    bd = _head_block_mask(hk, head)
    bd_f32 = jnp.where(bd, 1.0, 0.0).astype(F32)
    bd_bf16 = bd_f32.astype(BF16)
    st = st_ref[0]

    o = _dot_nt((q * jnp.exp(b)).astype(BF16), st.astype(BF16))

    nrep = hk // head
    att = jnp.zeros((rows, hk), F32)
    trow = lax.broadcasted_iota(jnp.int32, (rows, hk), 0)
    scol = lax.broadcasted_iota(jnp.int32, (rows, hk), 1) % rows
    m = rows // 2
    while m >= 8:
        span = 2 * m
        ref = jnp.concatenate(
            [jnp.broadcast_to(b[s * span + m - 1:s * span + m, :], (span, hk))
             for s in range(rows // span)], axis=0)
        second = (row // m) % 2 == 1
        qm = jnp.where(second, q * jnp.exp(jnp.minimum(b - ref, 0.0)), 0.0)
        km = jnp.where(second, 0.0, k * jnp.exp(jnp.minimum(ref - b, 0.0)))
        km_bd = jnp.where(bd, jnp.concatenate([km] * nrep, axis=0), 0.0)
        att_m = _dot_nt(qm.astype(BF16), km_bd.astype(BF16))
        att = att + jnp.where(trow // span == scol // span, att_m, 0.0)
        m //= 2
    v_bd = jnp.where(bd, jnp.concatenate([v] * nrep, axis=0), 0.0)
    o = o + _dot(att.astype(BF16), v_bd.astype(BF16))

    prods = []
    for j in range(8):
        if j == 0:
            prods.append(q * k)
        else:
            valid = (row % 8) >= j
            k_j = pltpu.roll(k, j, 0)
            b_j = pltpu.roll(b, j, 0)
            prods.append(jnp.where(valid, q * k_j * jnp.exp(jnp.minimum(b - b_j, 0.0)), 0.0))
    w = _dot(jnp.concatenate(prods, axis=0).astype(BF16), bd_bf16)
    for j in range(8):
        v_j = v if j == 0 else pltpu.roll(v, j, 0)
        o = o + w[j * rows:(j + 1) * rows] * v_j

    b_last = b[rows - 1:rows, :]
    kl = k * jnp.exp(b_last - b)
    upd = _dot_tn(v.astype(BF16), kl.astype(BF16))
    st_ref[0] = jnp.where(bd, st * jnp.exp(b_last) + upd, 0.0)

    yb_ref[0] = _hgrn_finish(o, g, ng_ref, bd_f32, head)


def _mixer_b_prompt(zb, st0, hg_lower, ng_tiled, layer):
    bsz, length, four_hk = zb.shape
    hk = four_hk // 4
    rows = 64
    assert length % rows == 0 and hk // B_HEADS == rows
    return pl.pallas_call(
        functools.partial(_mixer_b_prompt_body, layer=layer, rows=rows, hk=hk, head=hk // B_HEADS),
        grid=(bsz, length // rows),
        in_specs=[pl.BlockSpec((1, rows, four_hk), lambda b, c: (b, c, 0)),
                  pl.BlockSpec((1, hk, hk), lambda b, c: (b, 0, 0)),
                  pl.BlockSpec(hg_lower.shape, lambda b, c: (0, 0)),
                  pl.BlockSpec((1, hk), lambda b, c: (0, 0))],
        out_specs=[pl.BlockSpec((1, rows, hk), lambda b, c: (b, c, 0)),
                   pl.BlockSpec((1, hk, hk), lambda b, c: (b, 0, 0))],
        out_shape=[jax.ShapeDtypeStruct((bsz, length, hk), F32),
                   jax.ShapeDtypeStruct((bsz, hk, hk), F32)],
        compiler_params=_cparams("parallel", "arbitrary"),
        name="mixer_b_prompt",
    )(zb, st0, hg_lower, ng_tiled)


def _mixer_b_step_body(zb_ref, vcol_ref, st0_ref, hl_ref, ng_ref, yb_ref, st_ref, *, layer, hk, head):
    lb = _hgrn_lower_bound(hl_ref, layer)
    q, lf, k, _, g = _hgrn_inputs(zb_ref[0], lb, hk)
    bd = _head_block_mask(hk, head)
    bd_f32 = jnp.where(bd, 1.0, 0.0).astype(F32)
    st = jnp.where(bd, st0_ref[0] * jnp.exp(lf) + vcol_ref[0] * k, 0.0)
    st_ref[0] = st
    q8 = jnp.broadcast_to(q, (8, hk))
    o = _dot_nt(q8, st, HIGHEST)
    g8 = jnp.broadcast_to(g, (8, hk))
    yb_ref[0] = _hgrn_finish(o, g8, ng_ref, bd_f32, head)[0:1]


def _mixer_b_step(zb, st0, hg_lower, ng_tiled, layer):
    n, four_hk = zb.shape
    hk = four_hk // 4
    vcol = zb[:, 2 * hk:3 * hk].reshape(n, hk, 1)
    yb, st = pl.pallas_call(
        functools.partial(_mixer_b_step_body, layer=layer, hk=hk, head=hk // B_HEADS),
        grid=(n,),
        in_specs=[pl.BlockSpec((1, 1, four_hk), lambda i: (i, 0, 0)),
                  pl.BlockSpec((1, hk, 1), lambda i: (i, 0, 0)),
                  pl.BlockSpec((1, hk, hk), lambda i: (i, 0, 0)),
                  pl.BlockSpec(hg_lower.shape, lambda i: (0, 0)),
                  pl.BlockSpec((1, hk), lambda i: (0, 0))],
        out_specs=[pl.BlockSpec((1, 1, hk), lambda i: (i, 0, 0)),
                   pl.BlockSpec((1, hk, hk), lambda i: (i, 0, 0))],
        out_shape=[jax.ShapeDtypeStruct((n, 1, hk), F32),
                   jax.ShapeDtypeStruct((n, hk, hk), F32)],
        compiler_params=_cparams("parallel"),
        name="mixer_b_step",
    )(zb.reshape(n, 1, four_hk), vcol, st0, hg_lower, ng_tiled)
    return yb.reshape(n, hk), st


def _state_to_block_diag(s):
    n, h, kd, vd = s.shape
    eye = jnp.eye(h, dtype=s.dtype)
    st = jnp.einsum("nhkv,hg->nhvgk", s, eye)
    return st.reshape(n, h * vd, h * kd)


def _state_from_block_diag(st, h):
    n, hv, hk = st.shape
    vd, kd = hv // h, hk // h
    st = st.reshape(n, h, vd, h, kd)
    diag = jnp.stack([st[:, i, :, i, :] for i in range(h)], axis=1)
    return jnp.swapaxes(diag, 2, 3)


def _alibi_slope(head):
    slope = 0.0
    for i in range(C_HEADS):
        slope = jnp.where(head == i, 2.0 ** (-8.0 * (i + 1) / C_HEADS), slope)
    return slope


def _diff_lambda(lq1_ref, lk1_ref, lq2_ref, lk2_ref, lam_init):
    s1 = jnp.sum(lq1_ref[...] * lk1_ref[...], axis=-1, keepdims=True)
    s2 = jnp.sum(lq2_ref[...] * lk2_ref[...], axis=-1, keepdims=True)
    return jnp.exp(s1) - jnp.exp(s2) + lam_init


def _attn_prompt_body(qi_ref, ki_ref, q_ref, k_ref, v_ref, lq1_ref, lk1_ref, lq2_ref, lk2_ref,
                      ng_ref, o_ref, m_ref, l_ref, acc_ref, *, tq, tk, cv, lam_init, scale):
    h = pl.program_id(1)
    p = pl.program_id(2)
    qi = qi_ref[p]
    ki = ki_ref[p]
    half = cv // 2

    @pl.when(ki == 0)
    def _():
        m_ref[...] = jnp.full(m_ref.shape, NEG_INF, F32)
        l_ref[...] = jnp.zeros(l_ref.shape, F32)
        acc_ref[...] = jnp.zeros(acc_ref.shape, F32)

    slope = _alibi_slope(h)
    q = q_ref[0] * scale
    lane = lax.broadcasted_iota(jnp.int32, q.shape, 1)
    kb = k_ref[0].astype(BF16)
    vb = v_ref[0].astype(BF16)
    qpos = qi * tq + lax.broadcasted_iota(jnp.int32, (tq, tk), 0)
    kpos = ki * tk + lax.broadcasted_iota(jnp.int32, (tq, tk), 1)
    bias = slope * (qpos - kpos).astype(F32)
    keep = kpos <= qpos
    for c in range(2):
        qc = jnp.where((lane < half) == (c == 0), q, 0.0).astype(BF16)
        s = _dot_nt(qc, kb) - bias
        s = jnp.where(keep, s, NEG_INF)
        m_old = m_ref[c]
        m_new = jnp.maximum(m_old, jnp.max(s, axis=-1, keepdims=True))
        alpha = jnp.exp(m_old - m_new)
        pexp = jnp.exp(s - m_new)
        l_ref[c] = alpha * l_ref[c] + jnp.sum(pexp, axis=-1, keepdims=True)
        acc_ref[c] = alpha * acc_ref[c] + _dot(pexp.astype(BF16), vb)
        m_ref[c] = m_new

    @pl.when(ki == (qi * tq + tq - 1) // tk)
    def _():
        lam = _diff_lambda(lq1_ref, lk1_ref, lq2_ref, lk2_ref, lam_init)
        o = acc_ref[0] / l_ref[0] - lam * (acc_ref[1] / l_ref[1])
        ms = jnp.mean(o * o, axis=-1, keepdims=True)
        o_ref[0] = o * lax.rsqrt(ms + RMS_EPS) * ng_ref[...] * (1.0 - lam_init)


def _attn_prompt(q, k, v, lq1, lk1, lq2, lk2, ng, lam_init):
    bsz, length, width = q.shape
    cv = width // C_HEADS
    tq = tk = _row_tile(length, 512)
    nq = length // tq
    pairs = [(i, j) for i in range(nq) for j in range(i + 1)]
    qi_tab = jnp.asarray([i for i, _ in pairs], jnp.int32)
    ki_tab = jnp.asarray([j for _, j in pairs], jnp.int32)
    const = lambda shape: pl.BlockSpec(shape, lambda b, h, p, qt, kt: (0,) * len(shape))
    grid_spec = pltpu.PrefetchScalarGridSpec(
        num_scalar_prefetch=2,
        grid=(bsz, C_HEADS, len(pairs)),
        in_specs=[pl.BlockSpec((1, tq, cv), lambda b, h, p, qt, kt: (b, qt[p], h)),
                  pl.BlockSpec((1, tk, cv), lambda b, h, p, qt, kt: (b, kt[p], h)),
                  pl.BlockSpec((1, tk, cv), lambda b, h, p, qt, kt: (b, kt[p], h)),
                  const(lq1.shape), const(lk1.shape), const(lq2.shape), const(lk2.shape),
                  const(ng.shape)],
        out_specs=pl.BlockSpec((1, tq, cv), lambda b, h, p, qt, kt: (b, qt[p], h)),
        scratch_shapes=[pltpu.VMEM((2, tq, 1), F32), pltpu.VMEM((2, tq, 1), F32),
                        pltpu.VMEM((2, tq, cv), F32)],
    )
    return pl.pallas_call(
        functools.partial(_attn_prompt_body, tq=tq, tk=tk, cv=cv, lam_init=lam_init,
                          scale=(cv // 2) ** -0.5),
        grid_spec=grid_spec,
        out_shape=jax.ShapeDtypeStruct((bsz, length, width), F32),
        compiler_params=_cparams("parallel", "parallel", "arbitrary"),
        name="attn_prompt",
    )(qi_tab, ki_tab, q, k, v, lq1, lk1, lq2, lk2, ng)


def _attn_paged_body(pt_ref, q_ref, kn_ref, vn_ref, lq1_ref, lk1_ref, lq2_ref, lk2_ref, ng_ref,
                     *refs, pages_per_step, page, width, cv, lam_init, scale, past):
    k_refs = refs[:pages_per_step]
    v_refs = refs[pages_per_step:2 * pages_per_step]
    o_ref, m_ref, l_ref, acc_ref = refs[2 * pages_per_step:]
    j = pl.program_id(1)
    nrow = 2 * C_HEADS
    half = cv // 2

    @pl.when(j == 0)
    def _():
        m_ref[...] = jnp.full(m_ref.shape, NEG_INF, F32)
        l_ref[...] = jnp.zeros(l_ref.shape, F32)
        acc_ref[...] = jnp.zeros(acc_ref.shape, F32)

    rr = lax.broadcasted_iota(jnp.int32, (nrow, width), 0)
    ll = lax.broadcasted_iota(jnp.int32, (nrow, width), 1)
    own_half = (ll // half) == rr
    own_head = (ll // cv) == (rr // 2)
    qbd = jnp.where(own_half, jnp.broadcast_to(q_ref[0] * scale, (nrow, width)), 0.0)
    qbd_bf16 = qbd.astype(BF16)
    hrow = lax.broadcasted_iota(jnp.int32, (nrow, 1), 0) // 2
    slope = _alibi_slope(hrow)

    span = pages_per_step * page
    kpos = j * span + lax.broadcasted_iota(jnp.int32, (nrow, span), 1)
    s = jnp.concatenate([_dot_nt(qbd_bf16, kr[0, 0].astype(BF16)) for kr in k_refs], axis=1)
    s = s - slope * (past - kpos).astype(F32)
    m_old = m_ref[...]
    m_new = jnp.maximum(m_old, jnp.max(s, axis=-1, keepdims=True))
    alpha = jnp.exp(m_old - m_new)
    pexp = jnp.exp(s - m_new)
    l_new = alpha * l_ref[...] + jnp.sum(pexp, axis=-1, keepdims=True)
    pb = pexp.astype(BF16)
    pv = _dot(pb[:, 0:page], v_refs[0][0, 0].astype(BF16))
    for i in range(1, pages_per_step):
        pv = pv + _dot(pb[:, i * page:(i + 1) * page], v_refs[i][0, 0].astype(BF16))
    acc_new = alpha * acc_ref[...] + jnp.where(own_head, pv, 0.0)
    m_ref[...] = m_new
    l_ref[...] = l_new
    acc_ref[...] = acc_new

    @pl.when(j == pl.num_programs(1) - 1)
    def _():
        s_new = jnp.sum(qbd * kn_ref[0], axis=-1, keepdims=True)
        m_fin = jnp.maximum(m_new, s_new)
        a2 = jnp.exp(m_new - m_fin)
        p_new = jnp.exp(s_new - m_fin)
        l_fin = a2 * l_new + p_new
        acc_fin = a2 * acc_new + jnp.where(own_head, p_new * vn_ref[0], 0.0)
        lam = _diff_lambda(lq1_ref, lk1_ref, lq2_ref, lk2_ref, lam_init)
        first = lax.broadcasted_iota(jnp.int32, (nrow, 1), 0) % 2 == 0
        coef = jnp.where(first, 1.0, -lam) / l_fin
        o = jnp.sum(coef * acc_fin, axis=0, keepdims=True)
        o8 = jnp.broadcast_to(o, (8, width))
        hb = jnp.where(_head_block_mask(width, cv), 1.0, 0.0).astype(F32)
        ms = _dot(o8 * o8, hb, HIGHEST) * (1.0 / cv)
        o_ref[0] = (o8 * lax.rsqrt(ms + RMS_EPS) * ng_ref[...] * (1.0 - lam_init))[0:1]


def _attn_paged(q, k_new, v_new, cache_k, cache_v, page_table, layer, lq1, lk1, lq2, lk2,
                ng_tiled, lam_init):
    n, width = q.shape
    cv = width // C_HEADS
    _, _, page, _ = cache_k.shape
    npages = page_table.shape[1]
    pps = 8 if npages % 8 == 0 else 1
    steps = npages // pps
    const = lambda shape: pl.BlockSpec(shape, lambda b, j, pt: (0,) * len(shape))
    row = pl.BlockSpec((1, 1, width), lambda b, j, pt: (b, 0, 0))

    def page_spec(i):
        return pl.BlockSpec((1, 1, page, width), lambda b, j, pt: (layer, pt[b, j * pps + i], 0, 0))

    grid_spec = pltpu.PrefetchScalarGridSpec(
        num_scalar_prefetch=1,
        grid=(n, steps),
        in_specs=[row, row, row, const(lq1.shape), const(lk1.shape), const(lq2.shape),
                  const(lk2.shape), const(ng_tiled.shape)]
                 + [page_spec(i) for i in range(pps)] + [page_spec(i) for i in range(pps)],
        out_specs=row,
        scratch_shapes=[pltpu.VMEM((2 * C_HEADS, 1), F32), pltpu.VMEM((2 * C_HEADS, 1), F32),
                        pltpu.VMEM((2 * C_HEADS, width), F32)],
    )
    out = pl.pallas_call(
        functools.partial(_attn_paged_body, pages_per_step=pps, page=page, width=width, cv=cv,
                          lam_init=lam_init, scale=(cv // 2) ** -0.5, past=npages * page),
        grid_spec=grid_spec,
        out_shape=jax.ShapeDtypeStruct((n, 1, width), F32),
        compiler_params=_cparams("parallel", "arbitrary"),
        name="attn_paged",
    )(page_table, q.reshape(n, 1, width), k_new.reshape(n, 1, width), v_new.reshape(n, 1, width),
      lq1, lk1, lq2, lk2, ng_tiled, *([cache_k] * pps), *([cache_v] * pps))
    return out.reshape(n, width)


def _route(logits):
    lane = lax.broadcasted_iota(jnp.int32, logits.shape, 1)
    is_g = lane < N_GROUPS
    lg = jnp.where(is_g, logits, NEG_INF)
    gmax = jnp.max(lg, axis=-1, keepdims=True)
    lane_f = lane.astype(F32)
    g_idx = jnp.min(jnp.where(lg == gmax, lane_f, float(N_GROUPS)), axis=-1, keepdims=True)
    p_top = 1.0 / jnp.sum(jnp.exp(lg - gmax), axis=-1, keepdims=True)
    lo = N_GROUPS + EXP_PER_GROUP * g_idx
    in_group = (lane_f >= lo) & (lane_f < lo + EXP_PER_GROUP)
    le = jnp.where(in_group, logits, NEG_INF)
    v1 = jnp.max(le, axis=-1, keepdims=True)
    i1 = jnp.min(jnp.where(le == v1, lane_f, 128.0), axis=-1, keepdims=True)
    le2 = jnp.where(lane_f == i1, NEG_INF, le)
    v2 = jnp.max(le2, axis=-1, keepdims=True)
    i2 = jnp.min(jnp.where(le2 == v2, lane_f, 128.0), axis=-1, keepdims=True)
    e2 = jnp.exp(v2 - v1)
    w1 = p_top / (1.0 + e2)
    w2 = p_top * e2 / (1.0 + e2)
    comb = jnp.where(lane_f == i1, w1, 0.0) + jnp.where(lane_f == i2, w2, 0.0)
    return comb[:, N_GROUPS:N_GROUPS + N_EXPERTS]


def _out_proj_body(ya_ref, yb_ref, yc_ref, x_ref, w_ref, g_ref, b_ref, wr_ref, br_ref,
                   x1_ref, comb_ref, *, alpha, wa, wb):
    mixed = (_dot(ya_ref[...].astype(BF16), w_ref[0:wa, :])
             + _dot(yb_ref[...].astype(BF16), w_ref[wa:wa + wb, :])
             + _dot(yc_ref[...].astype(BF16), w_ref[wa + wb:, :]))
    x1 = _layer_norm(alpha * x_ref[...] + mixed, g_ref[...], b_ref[...])
    x1_ref[...] = x1
    logits = _dot(x1, wr_ref[...], HIGHEST) + br_ref[...]
    comb_ref[...] = _route(logits)


def _out_proj(ya, yb, yc, x, w_bf16, g, b, wr, br, alpha):
    t, d = x.shape
    wa, wb, wc = ya.shape[1], yb.shape[1], yc.shape[1]
    tm = _row_tile(t, 512)
    rowspec = lambda w: pl.BlockSpec((tm, w), lambda i: (i, 0))
    const = lambda shape: pl.BlockSpec(shape, lambda i: (0,) * len(shape))
    return pl.pallas_call(
        functools.partial(_out_proj_body, alpha=alpha, wa=wa, wb=wb),
        grid=(t // tm,),
        in_specs=[rowspec(wa), rowspec(wb), rowspec(wc), rowspec(d), const(w_bf16.shape),
                  const((1, d)), const((1, d)), const(wr.shape), const(br.shape)],
        out_specs=[rowspec(d), rowspec(N_EXPERTS)],
        out_shape=[jax.ShapeDtypeStruct((t, d), F32), jax.ShapeDtypeStruct((t, N_EXPERTS), F32)],
        compiler_params=_cparams("parallel"),
        name="out_proj",
    )(ya, yb, yc, x, w_bf16, g, b, wr, br)


def _moe_body(x_ref, comb_ref, wg_ref, wu_ref, wd_ref, g_ref, b_ref, o_ref, acc_ref, *, alpha):
    e = pl.program_id(1)

    @pl.when(e == 0)
    def _():
        acc_ref[...] = jnp.zeros(acc_ref.shape, F32)

    xb = x_ref[...].astype(BF16)
    comb = comb_ref[...]
    lane = lax.broadcasted_iota(jnp.int32, comb.shape, 1)
    ce = jnp.sum(jnp.where(lane == e, comb, 0.0), axis=-1, keepdims=True)
    hid = _silu(_dot(xb, wg_ref[0])) * _dot(xb, wu_ref[0]) * ce
    acc_ref[...] += _dot(hid.astype(BF16), wd_ref[0])

    @pl.when(e == pl.num_programs(1) - 1)
    def _():
        o_ref[...] = _layer_norm(alpha * x_ref[...] + acc_ref[...], g_ref[...], b_ref[...])


def _moe(x, comb, wg_bf16, wu_bf16, wd_bf16, g, b, alpha):
    t, d = x.shape
    ne, _, f = wg_bf16.shape
    tm = _row_tile(t, 512)
    return pl.pallas_call(
        functools.partial(_moe_body, alpha=alpha),
        grid=(t // tm, ne),
        in_specs=[pl.BlockSpec((tm, d), lambda i, e: (i, 0)),
                  pl.BlockSpec((tm, ne), lambda i, e: (i, 0)),
                  pl.BlockSpec((1, d, f), lambda i, e: (e, 0, 0)),
                  pl.BlockSpec((1, d, f), lambda i, e: (e, 0, 0)),
                  pl.BlockSpec((1, f, d), lambda i, e: (e, 0, 0)),
                  pl.BlockSpec((1, d), lambda i, e: (0, 0)),
                  pl.BlockSpec((1, d), lambda i, e: (0, 0))],
        out_specs=pl.BlockSpec((tm, d), lambda i, e: (i, 0)),
        out_shape=jax.ShapeDtypeStruct((t, d), F32),
        scratch_shapes=[pltpu.VMEM((tm, d), F32)],
        compiler_params=_cparams("parallel", "arbitrary"),
        name="moe",
    )(x, comb, wg_bf16, wu_bf16, wd_bf16, g, b)


def _block_diag(w):
    n, c, _ = w.shape
    eye = jnp.eye(n, dtype=w.dtype)
    return jnp.einsum("ncd,nm->ncmd", w, eye).reshape(n * c, n * c)


def kernel(x_prompt, x_sample, cache_k, cache_v, state_conv, state_rglru, state_hgrn, page_table,
           emb_ln_g, emb_ln_b, w_in, conv_w, conv_b, rg_wr, rg_br, rg_wi, rg_bi, rg_lambda,
           hg_lower, hg_norm_g, dl_lq1, dl_lk1, dl_lq2, dl_lk2, dl_norm_g, w_out,
           ln1_g, ln1_b, ln2_g, ln2_b, moe_wg, moe_bg, moe_we, moe_be,
           ex_w_gate, ex_w_up, ex_w_down):
    depth = w_in.shape[0]
    bp, seq, d = x_prompt.shape
    bs, dseq, _ = x_sample.shape
    assert dseq == 1
    wa = conv_w.shape[-1]
    hk = hg_lower.shape[-1]
    wc = C_HEADS * dl_norm_g.shape[-1]
    splits = (2 * wa, 4 * hk, wc, wc, wc)
    alpha = (2 * depth) ** 0.25
    row = lambda v: v.reshape(1, -1)

    xp = x_prompt.reshape(bp * seq, d)
    xs = x_sample.reshape(bs, d)
    pool, page = cache_k.shape[1], cache_k.shape[2]
    ck = cache_k.reshape(depth, pool, page, wc)
    cvv = cache_v.reshape(depth, pool, page, wc)
    zeros_conv = jnp.zeros((bp, CONV_W - 1, wa), F32)
    zeros_h = jnp.zeros((bp, wa), F32)
    zeros_st = jnp.zeros((bp, hk, hk), F32)

    outs = {n: [] for n in ("conv_p", "conv_s", "lru_p", "lru_s", "hg_p", "hg_s",
                            "k_p", "k_s", "v_p", "v_s")}
    for l in range(depth):
        lam_init = 0.8 - 0.6 * math.exp(-0.3 * l)
        w_in_b = w_in[l].astype(BF16)
        w_out_b = w_out[l].astype(BF16)
        gate_w = jnp.concatenate([_block_diag(rg_wr[l]), _block_diag(rg_wi[l])], axis=1).astype(BF16)
        gate_b = jnp.concatenate([rg_br[l], rg_bi[l]]).reshape(1, -1)
        hg_ng = row(jnp.tile(hg_norm_g[l], B_HEADS))
        dl_ng = row(dl_norm_g[l])
        dl_ng_t = row(jnp.tile(dl_norm_g[l], C_HEADS))
        lq1, lk1, lq2, lk2 = row(dl_lq1[l]), row(dl_lk1[l]), row(dl_lq2[l]), row(dl_lk2[l])
        pad = jnp.zeros((d, 128 - N_GROUPS - N_EXPERTS), F32)
        wr = jnp.concatenate([moe_wg[l], moe_we[l], pad], axis=1)
        br = jnp.concatenate([moe_bg[l], moe_be[l], jnp.zeros((128 - N_GROUPS - N_EXPERTS,), F32)]).reshape(1, -1)
        wg_b, wu_b, wd_b = ex_w_gate[l].astype(BF16), ex_w_up[l].astype(BF16), ex_w_down[l].astype(BF16)
        a_params = (conv_w[l], row(conv_b[l]), gate_w, gate_b, row(rg_lambda[l]))

        def tail(ya, yb, yc, x):
            x1, comb = _out_proj(ya, yb, yc, x, w_out_b, row(ln1_g[l]), row(ln1_b[l]), wr, br, alpha)
            return _moe(x1, comb, wg_b, wu_b, wd_b, row(ln2_g[l]), row(ln2_b[l]), alpha)

        res = _proj_in(xp, row(emb_ln_g), row(emb_ln_b), w_in_b, splits, apply_ln=(l == 0))
        if l == 0:
            xp = res[0]
            res = res[1:]
        za, zb, qc, kc, vc = res
        ya, conv_new, h_new = _mixer_a_prompt(za.reshape(bp, seq, 2 * wa), zeros_conv, zeros_h, *a_params)
        yb, st_new = _mixer_b_prompt(zb.reshape(bp, seq, 4 * hk), zeros_st, hg_lower, hg_ng, l)
        yc = _attn_prompt(qc.reshape(bp, seq, wc), kc.reshape(bp, seq, wc), vc.reshape(bp, seq, wc),
                          lq1, lk1, lq2, lk2, dl_ng, lam_init)
        xp = tail(ya.reshape(bp * seq, wa), yb.reshape(bp * seq, hk), yc.reshape(bp * seq, wc), xp)
        outs["conv_p"].append(conv_new)
        outs["lru_p"].append(h_new.reshape(bp, wa))
        outs["hg_p"].append(_state_from_block_diag(st_new, B_HEADS))
        outs["k_p"].append(kc.reshape(bp, seq, C_HEADS, wc // C_HEADS))
        outs["v_p"].append(vc.reshape(bp, seq, C_HEADS, wc // C_HEADS))

        res = _proj_in(xs, row(emb_ln_g), row(emb_ln_b), w_in_b, splits, apply_ln=(l == 0))
        if l == 0:
            xs = res[0]
            res = res[1:]
        za, zb, qc, kc, vc = res
        ya, conv_new, h_new = _mixer_a_step(za, state_conv[l], state_rglru[l], *a_params)
        yb, st_new = _mixer_b_step(zb, _state_to_block_diag(state_hgrn[l]), hg_lower, hg_ng, l)
        yc = _attn_paged(qc, kc, vc, ck, cvv, page_table, l, lq1, lk1, lq2, lk2, dl_ng_t, lam_init)
        xs = tail(ya, yb, yc, xs)
        outs["conv_s"].append(conv_new)
        outs["lru_s"].append(h_new)
        outs["hg_s"].append(_state_from_block_diag(st_new, B_HEADS))
        outs["k_s"].append(kc.reshape(bs, 1, C_HEADS, wc // C_HEADS))
        outs["v_s"].append(vc.reshape(bs, 1, C_HEADS, wc // C_HEADS))

    stack = lambda n: jnp.stack(outs[n])
    return (xp.reshape(bp, seq, d), xs.reshape(bs, 1, d),
            stack("conv_p"), stack("conv_s"), stack("lru_p"), stack("lru_s"),
            stack("hg_p"), stack("hg_s"), stack("k_p"), stack("k_s"), stack("v_p"), stack("v_s"))
```

```python
import functools
import math

import jax
import jax.numpy as jnp
import numpy as np
from jax import lax
from jax.experimental import pallas as pl
from jax.experimental.pallas import tpu as pltpu

F32 = jnp.float32
BF16 = jnp.bfloat16

A_BLOCKS = 4
CONV_W = 4
LRU_C = 8.0
B_HEADS = 4
C_HEADS = 4
N_GROUPS = 4
EXP_PER_GROUP = 4
N_EXPERTS = N_GROUPS * EXP_PER_GROUP
LN_EPS = 1e-5
RMS_EPS = 1e-6
NEG_INF = float("-inf")
LOG2E = math.log2(math.e)

VMEM_LIMIT_BYTES = 56 * 1024 * 1024
HIGHEST = lax.Precision.HIGHEST


def _cparams(*sem):
    return pltpu.CompilerParams(dimension_semantics=sem, vmem_limit_bytes=VMEM_LIMIT_BYTES)


def _row_tile(n, want):
    return want if n % want == 0 else n


def _layer_norm(x, g, b):
    mu = jnp.mean(x, axis=-1, keepdims=True)
    xc = x - mu
    var = jnp.mean(xc * xc, axis=-1, keepdims=True)
    return xc * lax.rsqrt(var + LN_EPS) * g + b


def _sigmoid(x):
    return 1.0 / (1.0 + jnp.exp(-x))


def _silu(x):
    return x * _sigmoid(x)


def _log_sigmoid(x):
    return jnp.minimum(x, 0.0) - jnp.log1p(jnp.exp(-jnp.abs(x)))


def _gelu_tanh(x):
    c = math.sqrt(2.0 / math.pi)
    return 0.5 * x * (1.0 + jnp.tanh(c * (x + 0.044715 * (x * x * x))))


def _dot(a, b, precision=None):
    return jnp.dot(a, b, preferred_element_type=F32, precision=precision)


def _dot_nt(a, b, precision=None):
    return lax.dot_general(a, b, (((1,), (1,)), ((), ())), preferred_element_type=F32,
                           precision=precision)


def _dot_tn(a, b, precision=None):
    return lax.dot_general(a, b, (((0,), (0,)), ((), ())), preferred_element_type=F32,
                           precision=precision)


def _shift_rows(x, d, fill):
    row = lax.broadcasted_iota(jnp.int32, x.shape, 0)
    return jnp.where(row >= d, pltpu.roll(x, d, 0), fill)


def _head_block_mask(n, head):
    r = lax.broadcasted_iota(jnp.int32, (n, n), 0) // head
    c = lax.broadcasted_iota(jnp.int32, (n, n), 1) // head
    return r == c


def _proj_in_body(x_ref, g_ref, b_ref, w_ref, *refs, apply_ln, splits, with_vt):
    if with_vt:
        wvt_ref, refs = refs[0], refs[1:]
    x = x_ref[...]
    if apply_ln:
        x = _layer_norm(x, g_ref[...], b_ref[...])
        refs[0][...] = x
        refs = refs[1:]
    xb = x.astype(BF16)
    off = 0
    for o_ref, width in zip(refs, splits):
        o_ref[...] = _dot(xb, w_ref[:, off:off + width])
        off += width
    if with_vt:
        refs[len(splits)][0] = _dot_nt(wvt_ref[...], xb)


def _proj_in(x, g, b, w_bf16, splits, apply_ln, wvt_bf16=None, seq=None):
    t, d = x.shape
    tm = _row_tile(t, 512)
    n = w_bf16.shape[1]
    const = lambda shape: pl.BlockSpec(shape, lambda i: (0,) * len(shape))
    args = [x, g, b, w_bf16]
    in_specs = [pl.BlockSpec((tm, d), lambda i: (i, 0)), const((1, d)), const((1, d)), const((d, n))]
    out_shape = [jax.ShapeDtypeStruct((t, s), F32) for s in splits]
    out_specs = [pl.BlockSpec((tm, s), lambda i: (i, 0)) for s in splits]
    if apply_ln:
        out_shape = [jax.ShapeDtypeStruct((t, d), F32)] + out_shape
        out_specs = [pl.BlockSpec((tm, d), lambda i: (i, 0))] + out_specs
    if wvt_bf16 is not None:
        assert seq % tm == 0
        nt = seq // tm
        wv = wvt_bf16.shape[0]
        args.append(wvt_bf16)
        in_specs.append(const(wvt_bf16.shape))
        out_shape.append(jax.ShapeDtypeStruct((t // seq, wv, seq), F32))
        out_specs.append(pl.BlockSpec((1, wv, tm), lambda i: (i // nt, 0, i % nt)))
    return pl.pallas_call(
        functools.partial(_proj_in_body, apply_ln=apply_ln, splits=splits,
                          with_vt=wvt_bf16 is not None),
        grid=(t // tm,),
        in_specs=in_specs,
        out_specs=out_specs,
        out_shape=out_shape,
        compiler_params=_cparams("parallel"),
        name="proj_in",
    )(*args)


def _rglru_gates(y, wg_ref, bg_ref, lam_ref, wa):
    gm = _dot(y.astype(BF16), wg_ref[...]) + bg_ref[...]
    r = _sigmoid(gm[:, :wa])
    i = _sigmoid(gm[:, wa:])
    log_a = LRU_C * r * _log_sigmoid(lam_ref[...])
    a = jnp.exp(log_a)
    th = jnp.tanh(log_a)
    u = jnp.sqrt(-2.0 * th / (1.0 - th)) * (i * y)
    return a, u


def _mixer_a_prompt_body(za_ref, cs_ref, h0_ref, cw_ref, cb_ref, wg_ref, bg_ref, lam_ref,
                         ya_ref, cn_ref, hn_ref, xbuf, *, tt, wa):
    t = pl.program_id(1)

    @pl.when(t == 0)
    def _():
        xbuf[5:8, :] = cs_ref[0]
        hn_ref[0] = h0_ref[0]

    x = za_ref[0, :, :wa]
    ga = za_ref[0, :, wa:]
    xbuf[8:8 + tt, :] = x
    cw = cw_ref[...]
    y = (cb_ref[...] + cw[0:1] * xbuf[5:5 + tt, :] + cw[1:2] * xbuf[6:6 + tt, :]
         + cw[2:3] * xbuf[7:7 + tt, :] + cw[3:4] * x)
    tail = x[tt - (CONV_W - 1):tt]
    xbuf[5:8, :] = tail
    cn_ref[0] = tail

    a, u = _rglru_gates(y, wg_ref, bg_ref, lam_ref, wa)
    d = 1
    while d < tt:
        a_s = _shift_rows(a, d, 1.0)
        u_s = _shift_rows(u, d, 0.0)
        u = a * u_s + u
        a = a * a_s
        d *= 2
    h = u + a * hn_ref[0]
    hn_ref[0] = h[tt - 1:tt]
    ya_ref[0] = h * _gelu_tanh(ga)


def _mixer_a_prompt(za, conv_state, h0, cw, cb, wg_bf16, bg, lam):
    bsz, length, two_wa = za.shape
    wa = two_wa // 2
    tt = _row_tile(length, 256)
    const = lambda shape: pl.BlockSpec(shape, lambda b, t: (0,) * len(shape))
    return pl.pallas_call(
        functools.partial(_mixer_a_prompt_body, tt=tt, wa=wa),
        grid=(bsz, length // tt),
        in_specs=[pl.BlockSpec((1, tt, two_wa), lambda b, t: (b, t, 0)),
                  pl.BlockSpec((1, CONV_W - 1, wa), lambda b, t: (b, 0, 0)),
                  pl.BlockSpec((1, 1, wa), lambda b, t: (b, 0, 0)),
                  const((CONV_W, wa)), const((1, wa)), const((wa, two_wa)), const((1, two_wa)),
                  const((1, wa))],
        out_specs=[pl.BlockSpec((1, tt, wa), lambda b, t: (b, t, 0)),
                   pl.BlockSpec((1, CONV_W - 1, wa), lambda b, t: (b, 0, 0)),
                   pl.BlockSpec((1, 1, wa), lambda b, t: (b, 0, 0))],
        out_shape=[jax.ShapeDtypeStruct((bsz, length, wa), F32),
                   jax.ShapeDtypeStruct((bsz, CONV_W - 1, wa), F32),
                   jax.ShapeDtypeStruct((bsz, 1, wa), F32)],
        scratch_shapes=[pltpu.VMEM((8 + tt, wa), F32)],
        compiler_params=_cparams("parallel", "arbitrary"),
        name="mixer_a_prompt",
    )(za, conv_state, h0.reshape(bsz, 1, wa), cw, cb, wg_bf16, bg, lam)


def _mixer_a_step_body(za_ref, c0_ref, c1_ref, c2_ref, h0_ref, cw_ref, cb_ref, wg_ref, bg_ref,
                       lam_ref, ya_ref, hn_ref, *, wa):
    x = za_ref[:, :wa]
    ga = za_ref[:, wa:]
    cw = cw_ref[...]
    y = (cb_ref[...] + cw[0:1] * c0_ref[...] + cw[1:2] * c1_ref[...] + cw[2:3] * c2_ref[...]
         + cw[3:4] * x)
    a, u = _rglru_gates(y, wg_ref, bg_ref, lam_ref, wa)
    h = u + a * h0_ref[...]
    hn_ref[...] = h
    ya_ref[...] = h * _gelu_tanh(ga)


def _mixer_a_step(za, conv_state, h0, cw, cb, wg_bf16, bg, lam):
    n, two_wa = za.shape
    wa = two_wa // 2
    ya, hn = pl.pallas_call(
        functools.partial(_mixer_a_step_body, wa=wa),
        out_shape=[jax.ShapeDtypeStruct((n, wa), F32), jax.ShapeDtypeStruct((n, wa), F32)],
        compiler_params=pltpu.CompilerParams(vmem_limit_bytes=VMEM_LIMIT_BYTES),
        name="mixer_a_step",
    )(za, conv_state[:, 0], conv_state[:, 1], conv_state[:, 2], h0, cw, cb, wg_bf16, bg, lam)
    conv_new = jnp.concatenate([conv_state[:, 1:], za[:, None, :wa]], axis=1)
    return ya, conv_new, hn


def _hgrn_lower_bound(hl_ref, layer):
    rows = [hl_ref[i:i + 1, :] for i in range(hl_ref.shape[0])]
    mx = functools.reduce(jnp.maximum, rows)
    es = [jnp.exp(r - mx) for r in rows]
    tot = functools.reduce(lambda p, q: p + q, es)
    ps = [e / tot for e in es]
    cum = ps[0]
    first = cum
    for i in range(1, layer + 1):
        cum = cum + ps[i]
    return cum - first


def _hgrn_inputs(z, lb, hk):
    q = _silu(z[:, 0:hk])
    fr = z[:, hk:2 * hk]
    v = z[:, 2 * hk:3 * hk]
    g = z[:, 3 * hk:4 * hk]
    log_lb = jnp.log(lb)
    c = jnp.log1p(-lb) + _log_sigmoid(fr)
    log_f = jnp.maximum(log_lb, c) + jnp.log1p(jnp.exp(-jnp.abs(log_lb - c)))
    k = (1.0 - lb) * _sigmoid(-fr)
    return q, log_f, k, v, g


def _hgrn_finish(o, g, ng_ref, bd_f32, head):
    ms = _dot(o * o, bd_f32, HIGHEST) * (1.0 / head)
    return o * lax.rsqrt(ms + RMS_EPS) * ng_ref[...] * _silu(g)


def _mixer_b_prompt_body(zb_ref, st0_ref, hl_ref, ng_ref, yb_ref, st_ref, *, layer, rows, hk, head):
    c = pl.program_id(1)

    @pl.when(c == 0)
    def _():
        st_ref[0] = st0_ref[0]

    lb = _hgrn_lower_bound(hl_ref, layer)
    q, lf, k, v, g = _hgrn_inputs(zb_ref[0], lb, hk)

    row = lax.broadcasted_iota(jnp.int32, (rows, hk), 0)
    b = lf
    d = 1
    while d < rows:
        b = b + _shift_rows(b, d, 0.0)
        d *= 2

    bd = _head_block_mask(hk, head)
    bd_f32 = jnp.where(bd, 1.0, 0.0).astype(F32)
    bd_bf16 = bd_f32.astype(BF16)
    st = st_ref[0]

    o = _dot_nt((q * jnp.exp(b)).astype(BF16), st.astype(BF16))

    nrep = hk // head
    att = jnp.zeros((rows, hk), F32)
    trow = lax.broadcasted_iota(jnp.int32, (rows, hk), 0)
    scol = lax.broadcasted_iota(jnp.int32, (rows, hk), 1) % rows
    m = rows // 2
    while m >= 8:
        span = 2 * m
        ref = jnp.concatenate(
            [jnp.broadcast_to(b[s * span + m - 1:s * span + m, :], (span, hk))
             for s in range(rows // span)], axis=0)
        second = (row // m) % 2 == 1
        qm = jnp.where(second, q * jnp.exp(jnp.minimum(b - ref, 0.0)), 0.0)
        km = jnp.where(second, 0.0, k * jnp.exp(jnp.minimum(ref - b, 0.0)))
        km_bd = jnp.where(bd, jnp.concatenate([km] * nrep, axis=0), 0.0)
        att_m = _dot_nt(qm.astype(BF16), km_bd.astype(BF16))
        att = att + jnp.where(trow // span == scol // span, att_m, 0.0)
        m //= 2
    v_bd = jnp.where(bd, jnp.concatenate([v] * nrep, axis=0), 0.0)
    o = o + _dot(att.astype(BF16), v_bd.astype(BF16))

    prods = []
    for j in range(8):
        if j == 0:
            prods.append(q * k)
        else:
            valid = (row % 8) >= j
            k_j = pltpu.roll(k, j, 0)
            b_j = pltpu.roll(b, j, 0)
            prods.append(jnp.where(valid, q * k_j * jnp.exp(jnp.minimum(b - b_j, 0.0)), 0.0))
    w = _dot(jnp.concatenate(prods, axis=0).astype(BF16), bd_bf16)
    for j in range(8):
        v_j = v if j == 0 else pltpu.roll(v, j, 0)
        o = o + w[j * rows:(j + 1) * rows] * v_j

    b_last = b[rows - 1:rows, :]
    kl = k * jnp.exp(b_last - b)
    upd = _dot_tn(v.astype(BF16), kl.astype(BF16))
    st_ref[0] = jnp.where(bd, st * jnp.exp(b_last) + upd, 0.0)

    yb_ref[0] = _hgrn_finish(o, g, ng_ref, bd_f32, head)


def _mixer_b_prompt(zb, st0, hg_lower, ng_tiled, layer):
    bsz, length, four_hk = zb.shape
    hk = four_hk // 4
    rows = 64
    assert length % rows == 0 and hk // B_HEADS == rows
    return pl.pallas_call(
        functools.partial(_mixer_b_prompt_body, layer=layer, rows=rows, hk=hk, head=hk // B_HEADS),
        grid=(bsz, length // rows),
        in_specs=[pl.BlockSpec((1, rows, four_hk), lambda b, c: (b, c, 0)),
                  pl.BlockSpec((1, hk, hk), lambda b, c: (b, 0, 0)),
                  pl.BlockSpec(hg_lower.shape, lambda b, c: (0, 0)),
                  pl.BlockSpec((1, hk), lambda b, c: (0, 0))],
        out_specs=[pl.BlockSpec((1, rows, hk), lambda b, c: (b, c, 0)),
                   pl.BlockSpec((1, hk, hk), lambda b, c: (b, 0, 0))],
        out_shape=[jax.ShapeDtypeStruct((bsz, length, hk), F32),
                   jax.ShapeDtypeStruct((bsz, hk, hk), F32)],
        compiler_params=_cparams("parallel", "arbitrary"),
        name="mixer_b_prompt",
    )(zb, st0, hg_lower, ng_tiled)


def _mixer_b_step_body(zb_ref, vcol_ref, st0_ref, hl_ref, ng_ref, yb_ref, st_ref, *, layer, hk, head):
    lb = _hgrn_lower_bound(hl_ref, layer)
    q, lf, k, _, g = _hgrn_inputs(zb_ref[0], lb, hk)
    bd = _head_block_mask(hk, head)
    bd_f32 = jnp.where(bd, 1.0, 0.0).astype(F32)
    st = jnp.where(bd, st0_ref[0] * jnp.exp(lf) + vcol_ref[0] * k, 0.0)
    st_ref[0] = st
    q8 = jnp.broadcast_to(q, (8, hk))
    o = _dot_nt(q8, st, HIGHEST)
    g8 = jnp.broadcast_to(g, (8, hk))
    yb_ref[0] = _hgrn_finish(o, g8, ng_ref, bd_f32, head)[0:1]


def _mixer_b_step(zb, st0, hg_lower, ng_tiled, layer):
    n, four_hk = zb.shape
    hk = four_hk // 4
    vcol = zb[:, 2 * hk:3 * hk].reshape(n, hk, 1)
    yb, st = pl.pallas_call(
        functools.partial(_mixer_b_step_body, layer=layer, hk=hk, head=hk // B_HEADS),
        grid=(n,),
        in_specs=[pl.BlockSpec((1, 1, four_hk), lambda i: (i, 0, 0)),
                  pl.BlockSpec((1, hk, 1), lambda i: (i, 0, 0)),
                  pl.BlockSpec((1, hk, hk), lambda i: (i, 0, 0)),
                  pl.BlockSpec(hg_lower.shape, lambda i: (0, 0)),
                  pl.BlockSpec((1, hk), lambda i: (0, 0))],
        out_specs=[pl.BlockSpec((1, 1, hk), lambda i: (i, 0, 0)),
                   pl.BlockSpec((1, hk, hk), lambda i: (i, 0, 0))],
        out_shape=[jax.ShapeDtypeStruct((n, 1, hk), F32),
                   jax.ShapeDtypeStruct((n, hk, hk), F32)],
        compiler_params=_cparams("parallel"),
        name="mixer_b_step",
    )(zb.reshape(n, 1, four_hk), vcol, st0, hg_lower, ng_tiled)
    return yb.reshape(n, hk), st


def _state_to_block_diag(s):
    n, h, kd, vd = s.shape
    eye = jnp.eye(h, dtype=s.dtype)
    st = jnp.einsum("nhkv,hg->nhvgk", s, eye)
    return st.reshape(n, h * vd, h * kd)


def _state_from_block_diag(st, h):
    n, hv, hk = st.shape
    vd, kd = hv // h, hk // h
    st = st.reshape(n, h, vd, h, kd)
    diag = jnp.stack([st[:, i, :, i, :] for i in range(h)], axis=1)
    return jnp.swapaxes(diag, 2, 3)


def _alibi_slope(head):
    slope = 0.0
    for i in range(C_HEADS):
        slope = jnp.where(head == i, 2.0 ** (-8.0 * (i + 1) / C_HEADS), slope)
    return slope


def _diff_lambda(lq1_ref, lk1_ref, lq2_ref, lk2_ref, lam_init):
    s1 = jnp.sum(lq1_ref[...] * lk1_ref[...], axis=-1, keepdims=True)
    s2 = jnp.sum(lq2_ref[...] * lk2_ref[...], axis=-1, keepdims=True)
    return jnp.exp(s1) - jnp.exp(s2) + lam_init


def _split_bf16(x, parts):
    out = []
    rest = np.asarray(x, np.float32)
    for _ in range(parts):
        piece = rest.astype(BF16).astype(np.float32)
        out.append(piece)
        rest = rest - piece
    return out


def _alibi_tables(length, cv):
    assert length <= 64 * 256
    pos = np.arange(length)
    kx = np.zeros((length, cv), np.float32)
    kx[:, 0:3] = (pos // 64)[:, None]
    kx[:, 3:6] = (pos % 64)[:, None]
    qx = np.zeros((C_HEADS, 16, cv), np.float32)
    for h in range(C_HEADS):
        c = np.float32(2.0 ** (-8.0 * (h + 1) / C_HEADS) * LOG2E)
        pieces = _split_bf16(c, 3)
        qx[h, :, 0:3] = [64.0 * p for p in pieces]
        qx[h, :, 3:6] = pieces
    return jnp.asarray(qx, BF16), jnp.asarray(kx, BF16)


def _attn_prompt_body(qi_ref, ki_ref, q_ref, k_ref, vt_ref, qx_ref, kx_ref, lq1_ref, lk1_ref,
                      lq2_ref, lk2_ref, ngc_ref, o_ref, q2_ref, m_ref, acc_ref,
                      *, tq, tk, cv, lam_init, scale):
    p = pl.program_id(2)
    qi = qi_ref[p]
    ki = ki_ref[p]
    half = cv // 2

    @pl.when(ki == 0)
    def _():
        m_ref[...] = jnp.full(m_ref.shape, NEG_INF, F32)
        acc_ref[...] = jnp.zeros(acc_ref.shape, F32)
        q = q_ref[0] * (scale * LOG2E)
        lane = lax.broadcasted_iota(jnp.int32, q.shape, 1)
        qx = jnp.broadcast_to(qx_ref[0, 0:1, :], (tq, cv))
        for c in range(2):
            qc = jnp.where((lane < half) == (c == 0), q, 0.0).astype(BF16)
            q2_ref[c] = jnp.concatenate([qc, qx], axis=1)

    kb = jnp.concatenate([k_ref[0].astype(BF16), kx_ref[...]], axis=1)
    sub = lax.broadcasted_iota(jnp.int32, (16, tk), 0)
    ones = jnp.where(sub == 0, 1.0, 0.0).astype(BF16)
    vtb = jnp.concatenate([vt_ref[0].astype(BF16), ones], axis=0)

    def tile(masked):
        for c in range(2):
            st = _dot_nt(kb, q2_ref[c])
            if masked:
                key = lax.broadcasted_iota(jnp.int32, (tk, tq), 0)
                qry = lax.broadcasted_iota(jnp.int32, (tk, tq), 1)
                st = jnp.where(key <= qry, st, NEG_INF)
            m_old = m_ref[c]
            m_new = jnp.maximum(m_old, jnp.max(st, axis=0, keepdims=True))
            pt = jnp.exp2(st - m_new).astype(BF16)
            acc_ref[c] = jnp.exp2(m_old - m_new) * acc_ref[c] + _dot(vtb, pt)
            m_ref[c] = m_new

    @pl.when(ki < qi)
    def _():
        tile(False)

    @pl.when(ki == qi)
    def _():
        tile(True)
        lam = _diff_lambda(lq1_ref, lk1_ref, lq2_ref, lk2_ref, lam_init)
        a1 = acc_ref[0]
        a2 = acc_ref[1]
        ot = a1[:cv] / a1[cv:cv + 1] - lam * (a2[:cv] / a2[cv:cv + 1])
        ms = jnp.mean(ot * ot, axis=0, keepdims=True)
        ot = ot * lax.rsqrt(ms + RMS_EPS) * ngc_ref[...] * (1.0 - lam_init)
        o_ref[0] = ot.T


def _attn_prompt(q, k, vt, lq1, lk1, lq2, lk2, ng, lam_init):
    bsz, length, width = q.shape
    cv = width // C_HEADS
    tq = tk = _row_tile(length, 512)
    nq = length // tq
    pairs = [(i, j) for i in range(nq) for j in range(i + 1)]
    qi_tab = jnp.asarray([i for i, _ in pairs], jnp.int32)
    ki_tab = jnp.asarray([j for _, j in pairs], jnp.int32)
    qx, kx = _alibi_tables(length, cv)
    const = lambda shape: pl.BlockSpec(shape, lambda b, h, p, qt, kt: (0,) * len(shape))
    grid_spec = pltpu.PrefetchScalarGridSpec(
        num_scalar_prefetch=2,
        grid=(bsz, C_HEADS, len(pairs)),
        in_specs=[pl.BlockSpec((1, tq, cv), lambda b, h, p, qt, kt: (b, qt[p], h)),
                  pl.BlockSpec((1, tk, cv), lambda b, h, p, qt, kt: (b, kt[p], h)),
                  pl.BlockSpec((1, cv, tk), lambda b, h, p, qt, kt: (b, h, kt[p])),
                  pl.BlockSpec((1, 16, cv), lambda b, h, p, qt, kt: (h, 0, 0)),
                  pl.BlockSpec((tk, cv), lambda b, h, p, qt, kt: (kt[p], 0)),
                  const(lq1.shape), const(lk1.shape), const(lq2.shape), const(lk2.shape),
                  const((cv, 1))],
        out_specs=pl.BlockSpec((1, tq, cv), lambda b, h, p, qt, kt: (b, qt[p], h)),
        scratch_shapes=[pltpu.VMEM((2, tq, 2 * cv), BF16), pltpu.VMEM((2, 1, tq), F32),
                        pltpu.VMEM((2, cv + 16, tq), F32)],
    )
    return pl.pallas_call(
        functools.partial(_attn_prompt_body, tq=tq, tk=tk, cv=cv, lam_init=lam_init,
                          scale=(cv // 2) ** -0.5),
        grid_spec=grid_spec,
        out_shape=jax.ShapeDtypeStruct((bsz, length, width), F32),
        compiler_params=_cparams("parallel", "parallel", "arbitrary"),
        name="attn_prompt",
    )(qi_tab, ki_tab, q, k, vt, qx, kx, lq1, lk1, lq2, lk2, ng.reshape(cv, 1))


def _attn_paged_body(pt_ref, q_ref, kn_ref, vn_ref, lq1_ref, lk1_ref, lq2_ref, lk2_ref, ng_ref,
                     *refs, pages_per_step, groups, page, cv, lam_init, scale, past):
    k_refs = refs[:pages_per_step]
    v_refs = refs[pages_per_step:2 * pages_per_step]
    o_ref, m_ref, l_ref, acc_ref = refs[2 * pages_per_step:]
    j = pl.program_id(1)
    nrow = 2 * C_HEADS
    half = cv // 2
    prow = page * C_HEADS
    gp = pages_per_step // groups

    @pl.when(j == 0)
    def _():
        m_ref[...] = jnp.full(m_ref.shape, NEG_INF, F32)
        l_ref[...] = jnp.zeros(l_ref.shape, F32)
        acc_ref[...] = jnp.zeros(acc_ref.shape, F32)

    row = lax.broadcasted_iota(jnp.int32, (nrow, 1), 0)
    lane = lax.broadcasted_iota(jnp.int32, (nrow, cv), 1)
    qbd = jnp.where((lane // half) == (row % 2), q_ref[0] * scale, 0.0)
    qbd_bf16 = qbd.astype(BF16)
    hrow = row // 2
    slope = _alibi_slope(hrow)

    col = lax.broadcasted_iota(jnp.int32, (nrow, gp * prow), 1)
    same_head = (col % C_HEADS) == hrow
    for g in range(groups):
        pages = range(g * gp, (g + 1) * gp)
        s = jnp.concatenate([_dot_nt(qbd_bf16, k_refs[i][0, 0].astype(BF16)) for i in pages], axis=1)
        kpos = (j * pages_per_step + g * gp) * page + col // C_HEADS
        s = jnp.where(same_head, s - slope * (past - kpos).astype(F32), NEG_INF)
        m_old = m_ref[g]
        m_new = jnp.maximum(m_old, jnp.max(s, axis=-1, keepdims=True))
        alpha = jnp.exp(m_old - m_new)
        pexp = jnp.exp(s - m_new)
        l_ref[g] = alpha * l_ref[g] + jnp.sum(pexp, axis=-1, keepdims=True)
        pb = pexp.astype(BF16)
        pv = None
        for n, i in enumerate(pages):
            term = _dot(pb[:, n * prow:(n + 1) * prow], v_refs[i][0, 0].astype(BF16))
            pv = term if pv is None else pv + term
        acc_ref[g] = alpha * acc_ref[g] + pv
        m_ref[g] = m_new

    @pl.when(j == pl.num_programs(1) - 1)
    def _():
        s_new = jnp.sum(qbd * kn_ref[0], axis=-1, keepdims=True)
        m_fin = s_new
        for g in range(groups):
            m_fin = jnp.maximum(m_fin, m_ref[g])
        p_new = jnp.exp(s_new - m_fin)
        l_fin = p_new
        acc_fin = p_new * vn_ref[0]
        for g in range(groups):
            w = jnp.exp(m_ref[g] - m_fin)
            l_fin = l_fin + w * l_ref[g]
            acc_fin = acc_fin + w * acc_ref[g]
        lam = _diff_lambda(lq1_ref, lk1_ref, lq2_ref, lk2_ref, lam_init)
        d = jnp.where(row % 2 == 0, 1.0, -lam) / l_fin * acc_fin
        o = d + pltpu.roll(d, nrow - 1, 0)
        ms = jnp.mean(o * o, axis=-1, keepdims=True)
        o_ref[0] = o * lax.rsqrt(ms + RMS_EPS) * ng_ref[...] * (1.0 - lam_init)


def _attn_paged(q, k_new, v_new, cache_k, cache_v, page_table, layer, lq1, lk1, lq2, lk2,
                ng, lam_init):
    n, width = q.shape
    cv = width // C_HEADS
    prow = cache_k.shape[2]
    page = prow // C_HEADS
    npages = page_table.shape[1]
    pps = next(c for c in (16, 8, 4, 2, 1) if npages % c == 0)
    groups = 2 if pps % 2 == 0 else 1
    nrow = 2 * C_HEADS
    const = lambda shape: pl.BlockSpec(shape, lambda b, j, pt: (0,) * len(shape))
    rows = pl.BlockSpec((1, nrow, cv), lambda b, j, pt: (b, 0, 0))
    per_half = lambda t: jnp.repeat(t.reshape(n, C_HEADS, cv), 2, axis=1)

    def page_spec(i):
        return pl.BlockSpec((1, 1, prow, cv), lambda b, j, pt: (layer, pt[b, j * pps + i], 0, 0))

    grid_spec = pltpu.PrefetchScalarGridSpec(
        num_scalar_prefetch=1,
        grid=(n, npages // pps),
        in_specs=[rows, rows, rows, const(lq1.shape), const(lk1.shape), const(lq2.shape),
                  const(lk2.shape), const(ng.shape)]
                 + [page_spec(i) for i in range(pps)] + [page_spec(i) for i in range(pps)],
        out_specs=rows,
        scratch_shapes=[pltpu.VMEM((groups, nrow, 1), F32), pltpu.VMEM((groups, nrow, 1), F32),
                        pltpu.VMEM((groups, nrow, cv), F32)],
    )
    out = pl.pallas_call(
        functools.partial(_attn_paged_body, pages_per_step=pps, groups=groups, page=page, cv=cv,
                          lam_init=lam_init, scale=(cv // 2) ** -0.5, past=npages * page),
        grid_spec=grid_spec,
        out_shape=jax.ShapeDtypeStruct((n, nrow, cv), F32),
        compiler_params=_cparams("parallel", "arbitrary"),
        name="attn_paged",
    )(page_table, per_half(q), per_half(k_new), per_half(v_new),
      lq1, lk1, lq2, lk2, ng, *([cache_k] * pps), *([cache_v] * pps))
    return out[:, 0::2, :].reshape(n, width)


def _route(logits):
    lane = lax.broadcasted_iota(jnp.int32, logits.shape, 1)
    is_g = lane < N_GROUPS
    lg = jnp.where(is_g, logits, NEG_INF)
    gmax = jnp.max(lg, axis=-1, keepdims=True)
    lane_f = lane.astype(F32)
    g_idx = jnp.min(jnp.where(lg == gmax, lane_f, float(N_GROUPS)), axis=-1, keepdims=True)
    p_top = 1.0 / jnp.sum(jnp.exp(lg - gmax), axis=-1, keepdims=True)
    lo = N_GROUPS + EXP_PER_GROUP * g_idx
    in_group = (lane_f >= lo) & (lane_f < lo + EXP_PER_GROUP)
    le = jnp.where(in_group, logits, NEG_INF)
    v1 = jnp.max(le, axis=-1, keepdims=True)
    i1 = jnp.min(jnp.where(le == v1, lane_f, 128.0), axis=-1, keepdims=True)
    le2 = jnp.where(lane_f == i1, NEG_INF, le)
    v2 = jnp.max(le2, axis=-1, keepdims=True)
    i2 = jnp.min(jnp.where(le2 == v2, lane_f, 128.0), axis=-1, keepdims=True)
    e2 = jnp.exp(v2 - v1)
    w1 = p_top / (1.0 + e2)
    w2 = p_top * e2 / (1.0 + e2)
    comb = jnp.where(lane_f == i1, w1, 0.0) + jnp.where(lane_f == i2, w2, 0.0)
    return comb[:, N_GROUPS:N_GROUPS + N_EXPERTS]


def _out_proj_body(ya_ref, yb_ref, yc_ref, x_ref, w_ref, g_ref, b_ref, wr_ref, br_ref,
                   x1_ref, comb_ref, *, alpha, wa, wb):
    mixed = (_dot(ya_ref[...].astype(BF16), w_ref[0:wa, :])
             + _dot(yb_ref[...].astype(BF16), w_ref[wa:wa + wb, :])
             + _dot(yc_ref[...].astype(BF16), w_ref[wa + wb:, :]))
    x1 = _layer_norm(alpha * x_ref[...] + mixed, g_ref[...], b_ref[...])
    x1_ref[...] = x1
    nl = br_ref.shape[1]
    x1h = x1.astype(BF16)
    x1l = (x1 - x1h.astype(F32)).astype(BF16)
    hh = _dot(x1h, wr_ref[...])
    logits = hh[:, :nl] + hh[:, nl:] + _dot(x1l, wr_ref[:, :nl]) + br_ref[...]
    comb_ref[...] = _route(logits)


def _out_proj(ya, yb, yc, x, w_bf16, g, b, wr, br, alpha):
    t, d = x.shape
    wa, wb, wc = ya.shape[1], yb.shape[1], yc.shape[1]
    tm = _row_tile(t, 512)
    rowspec = lambda w: pl.BlockSpec((tm, w), lambda i: (i, 0))
    const = lambda shape: pl.BlockSpec(shape, lambda i: (0,) * len(shape))
    return pl.pallas_call(
        functools.partial(_out_proj_body, alpha=alpha, wa=wa, wb=wb),
        grid=(t // tm,),
        in_specs=[rowspec(wa), rowspec(wb), rowspec(wc), rowspec(d), const(w_bf16.shape),
                  const((1, d)), const((1, d)), const(wr.shape), const(br.shape)],
        out_specs=[rowspec(d), rowspec(N_EXPERTS)],
        out_shape=[jax.ShapeDtypeStruct((t, d), F32), jax.ShapeDtypeStruct((t, N_EXPERTS), F32)],
        compiler_params=_cparams("parallel"),
        name="out_proj",
    )(ya, yb, yc, x, w_bf16, g, b, wr, br)


def _moe_body(x_ref, comb_ref, wg_ref, wu_ref, wd_ref, g_ref, b_ref, o_ref, xb_ref, acc_ref,
              *, alpha, per_step):
    s = pl.program_id(1)

    @pl.when(s == 0)
    def _():
        acc_ref[...] = jnp.zeros(acc_ref.shape, F32)
        xb_ref[...] = x_ref[...].astype(BF16)

    xb = xb_ref[...]
    comb = comb_ref[...]
    lane = lax.broadcasted_iota(jnp.int32, comb.shape, 1)
    hids = []
    for i in range(per_step):
        ce = jnp.sum(jnp.where(lane == s * per_step + i, comb, 0.0), axis=-1, keepdims=True)
        hid = _silu(_dot(xb, wg_ref[i])) * _dot(xb, wu_ref[i]) * ce
        hids.append(hid.astype(BF16))
    acc_ref[...] += _dot(jnp.concatenate(hids, axis=1), wd_ref[...])

    @pl.when(s == pl.num_programs(1) - 1)
    def _():
        o_ref[...] = _layer_norm(alpha * x_ref[...] + acc_ref[...], g_ref[...], b_ref[...])


def _moe(x, comb, wg_bf16, wu_bf16, wd_bf16, g, b, alpha):
    t, d = x.shape
    ne, _, f = wg_bf16.shape
    tm = _row_tile(t, 1024)
    per_step = EXP_PER_GROUP
    return pl.pallas_call(
        functools.partial(_moe_body, alpha=alpha, per_step=per_step),
        grid=(t // tm, ne // per_step),
        in_specs=[pl.BlockSpec((tm, d), lambda i, s: (i, 0)),
                  pl.BlockSpec((tm, ne), lambda i, s: (i, 0)),
                  pl.BlockSpec((per_step, d, f), lambda i, s: (s, 0, 0)),
                  pl.BlockSpec((per_step, d, f), lambda i, s: (s, 0, 0)),
                  pl.BlockSpec((per_step * f, d), lambda i, s: (s, 0)),
                  pl.BlockSpec((1, d), lambda i, s: (0, 0)),
                  pl.BlockSpec((1, d), lambda i, s: (0, 0))],
        out_specs=pl.BlockSpec((tm, d), lambda i, s: (i, 0)),
        out_shape=jax.ShapeDtypeStruct((t, d), F32),
        scratch_shapes=[pltpu.VMEM((tm, d), BF16), pltpu.VMEM((tm, d), F32)],
        compiler_params=_cparams("parallel", "arbitrary"),
        name="moe",
    )(x, comb, wg_bf16, wu_bf16, wd_bf16, g, b)


def _block_diag(w):
    n, c, _ = w.shape
    eye = jnp.eye(n, dtype=w.dtype)
    return jnp.einsum("ncd,nm->ncmd", w, eye).reshape(n * c, n * c)


def kernel(x_prompt, x_sample, cache_k, cache_v, state_conv, state_rglru, state_hgrn, page_table,
           emb_ln_g, emb_ln_b, w_in, conv_w, conv_b, rg_wr, rg_br, rg_wi, rg_bi, rg_lambda,
           hg_lower, hg_norm_g, dl_lq1, dl_lk1, dl_lq2, dl_lk2, dl_norm_g, w_out,
           ln1_g, ln1_b, ln2_g, ln2_b, moe_wg, moe_bg, moe_we, moe_be,
           ex_w_gate, ex_w_up, ex_w_down):
    depth = w_in.shape[0]
    bp, seq, d = x_prompt.shape
    bs, dseq, _ = x_sample.shape
    assert dseq == 1
    wa = conv_w.shape[-1]
    hk = hg_lower.shape[-1]
    wc = C_HEADS * dl_norm_g.shape[-1]
    splits = (2 * wa, 4 * hk, wc, wc, wc)
    alpha = (2 * depth) ** 0.25
    row = lambda v: v.reshape(1, -1)

    xp = x_prompt.reshape(bp * seq, d)
    xs = x_sample.reshape(bs, d)
    pool, page = cache_k.shape[1], cache_k.shape[2]
    ck = cache_k.reshape(depth, pool, page * C_HEADS, wc // C_HEADS)
    cvv = cache_v.reshape(depth, pool, page * C_HEADS, wc // C_HEADS)
    zeros_conv = jnp.zeros((bp, CONV_W - 1, wa), F32)
    zeros_h = jnp.zeros((bp, wa), F32)
    zeros_st = jnp.zeros((bp, hk, hk), F32)

    outs = {n: [] for n in ("conv_p", "conv_s", "lru_p", "lru_s", "hg_p", "hg_s",
                            "k_p", "k_s", "v_p", "v_s")}
    for l in range(depth):
        lam_init = 0.8 - 0.6 * math.exp(-0.3 * l)
        w_in_b = w_in[l].astype(BF16)
        w_out_b = w_out[l].astype(BF16)
        gate_w = jnp.concatenate([_block_diag(rg_wr[l]), _block_diag(rg_wi[l])], axis=1).astype(BF16)
        gate_b = jnp.concatenate([rg_br[l], rg_bi[l]]).reshape(1, -1)
        hg_ng = row(jnp.tile(hg_norm_g[l], B_HEADS))
        dl_ng = row(dl_norm_g[l])
        lq1, lk1, lq2, lk2 = row(dl_lq1[l]), row(dl_lk1[l]), row(dl_lq2[l]), row(dl_lk2[l])
        pad = jnp.zeros((d, 128 - N_GROUPS - N_EXPERTS), F32)
        wr = jnp.concatenate([moe_wg[l], moe_we[l], pad], axis=1)
        wr_hi = wr.astype(BF16)
        wr = jnp.concatenate([wr_hi, (wr - wr_hi.astype(F32)).astype(BF16)], axis=1)
        br = jnp.concatenate([moe_bg[l], moe_be[l], jnp.zeros((128 - N_GROUPS - N_EXPERTS,), F32)]).reshape(1, -1)
        wg_b, wu_b = ex_w_gate[l].astype(BF16), ex_w_up[l].astype(BF16)
        wd_b = ex_w_down[l].astype(BF16).reshape(-1, d)
        a_params = (conv_w[l], row(conv_b[l]), gate_w, gate_b, row(rg_lambda[l]))

        def tail(ya, yb, yc, x):
            x1, comb = _out_proj(ya, yb, yc, x, w_out_b, row(ln1_g[l]), row(ln1_b[l]), wr, br, alpha)
            return _moe(x1, comb, wg_b, wu_b, wd_b, row(ln2_g[l]), row(ln2_b[l]), alpha)

        res = _proj_in(xp, row(emb_ln_g), row(emb_ln_b), w_in_b, splits, apply_ln=(l == 0),
                       wvt_bf16=w_in_b[:, sum(splits) - wc:].T, seq=seq)
        if l == 0:
            xp = res[0]
            res = res[1:]
        za, zb, qc, kc, vc, vct = res
        ya, conv_new, h_new = _mixer_a_prompt(za.reshape(bp, seq, 2 * wa), zeros_conv, zeros_h, *a_params)
        yb, st_new = _mixer_b_prompt(zb.reshape(bp, seq, 4 * hk), zeros_st, hg_lower, hg_ng, l)
        yc = _attn_prompt(qc.reshape(bp, seq, wc), kc.reshape(bp, seq, wc), vct,
                          lq1, lk1, lq2, lk2, dl_ng, lam_init)
        xp = tail(ya.reshape(bp * seq, wa), yb.reshape(bp * seq, hk), yc.reshape(bp * seq, wc), xp)
        outs["conv_p"].append(conv_new)
        outs["lru_p"].append(h_new.reshape(bp, wa))
        outs["hg_p"].append(_state_from_block_diag(st_new, B_HEADS))
        outs["k_p"].append(kc.reshape(bp, seq, C_HEADS, wc // C_HEADS))
        outs["v_p"].append(vc.reshape(bp, seq, C_HEADS, wc // C_HEADS))

        res = _proj_in(xs, row(emb_ln_g), row(emb_ln_b), w_in_b, splits, apply_ln=(l == 0))
        if l == 0:
            xs = res[0]
            res = res[1:]
        za, zb, qc, kc, vc = res
        ya, conv_new, h_new = _mixer_a_step(za, state_conv[l], state_rglru[l], *a_params)
        yb, st_new = _mixer_b_step(zb, _state_to_block_diag(state_hgrn[l]), hg_lower, hg_ng, l)
        yc = _attn_paged(qc, kc, vc, ck, cvv, page_table, l, lq1, lk1, lq2, lk2, dl_ng, lam_init)
        xs = tail(ya, yb, yc, xs)
        outs["conv_s"].append(conv_new)
        outs["lru_s"].append(h_new)
        outs["hg_s"].append(_state_from_block_diag(st_new, B_HEADS))
        outs["k_s"].append(kc.reshape(bs, 1, C_HEADS, wc // C_HEADS))
        outs["v_s"].append(vc.reshape(bs, 1, C_HEADS, wc // C_HEADS))

    stack = lambda n: jnp.stack(outs[n])
    return (xp.reshape(bp, seq, d), xs.reshape(bs, 1, d),
            stack("conv_p"), stack("conv_s"), stack("lru_p"), stack("lru_s"),
            stack("hg_p"), stack("hg_s"), stack("k_p"), stack("k_s"), stack("v_p"), stack("v_s"))
```

```python
import functools
import math

import jax
import jax.numpy as jnp
import numpy as np
from jax import lax
from jax.experimental import pallas as pl
from jax.experimental.pallas import tpu as pltpu

F32 = jnp.float32
BF16 = jnp.bfloat16

A_BLOCKS = 4
CONV_W = 4
LRU_C = 8.0
B_HEADS = 4
C_HEADS = 4
N_GROUPS = 4
EXP_PER_GROUP = 4
N_EXPERTS = N_GROUPS * EXP_PER_GROUP
LN_EPS = 1e-5
RMS_EPS = 1e-6
NEG_INF = float("-inf")
LOG2E = math.log2(math.e)

VMEM_LIMIT_BYTES = 56 * 1024 * 1024
HIGHEST = lax.Precision.HIGHEST


def _cparams(*sem):
    return pltpu.CompilerParams(dimension_semantics=sem, vmem_limit_bytes=VMEM_LIMIT_BYTES)


def _row_tile(n, want):
    return want if n % want == 0 else n


def _layer_norm(x, g, b):
    mu = jnp.mean(x, axis=-1, keepdims=True)
    xc = x - mu
    var = jnp.mean(xc * xc, axis=-1, keepdims=True)
    return xc * lax.rsqrt(var + LN_EPS) * g + b


def _sigmoid(x):
    return 0.5 * jnp.tanh(0.5 * x) + 0.5


def _silu(x):
    return x * _sigmoid(x)


def _log_sigmoid(x):
    return jnp.minimum(x, 0.0) - jnp.log1p(jnp.exp(-jnp.abs(x)))


def _log1p_exp(x):
    return jnp.log(1.0 + jnp.exp(x))


def _gelu_tanh(x):
    c = math.sqrt(2.0 / math.pi)
    return 0.5 * x * (1.0 + jnp.tanh(c * (x + 0.044715 * (x * x * x))))


def _dot(a, b, precision=None):
    return jnp.dot(a, b, preferred_element_type=F32, precision=precision)


def _dot_nt(a, b, precision=None):
    return lax.dot_general(a, b, (((1,), (1,)), ((), ())), preferred_element_type=F32,
                           precision=precision)


def _dot_tn(a, b, precision=None):
    return lax.dot_general(a, b, (((0,), (0,)), ((), ())), preferred_element_type=F32,
                           precision=precision)


def _shift_rows(x, d, fill):
    row = lax.broadcasted_iota(jnp.int32, x.shape, 0)
    return jnp.where(row >= d, pltpu.roll(x, d, 0), fill)


def _head_block_mask(n, head):
    r = lax.broadcasted_iota(jnp.int32, (n, n), 0) // head
    c = lax.broadcasted_iota(jnp.int32, (n, n), 1) // head
    return r == c


def _mm(a, w, precise):
    if precise:
        return _dot(a.astype(F32), w, HIGHEST)
    return _dot(a.astype(BF16), w)


def _proj_in_step_body(x_ref, g_ref, b_ref, w_ref, *out_refs, apply_ln, splits):
    x = x_ref[...]
    if apply_ln:
        x = _layer_norm(x, g_ref[...], b_ref[...])
        out_refs[0][...] = x
        out_refs = out_refs[1:]
    off = 0
    for o_ref, width in zip(out_refs, splits):
        o_ref[...] = _mm(x, w_ref[:, off:off + width], True)
        off += width


def _proj_in_step(x, g, b, w_f32, splits, apply_ln):
    t, d = x.shape
    out_shape = [jax.ShapeDtypeStruct((t, s), F32) for s in splits]
    if apply_ln:
        out_shape = [jax.ShapeDtypeStruct((t, d), F32)] + out_shape
    return pl.pallas_call(
        functools.partial(_proj_in_step_body, apply_ln=apply_ln, splits=splits),
        out_shape=out_shape,
        compiler_params=pltpu.CompilerParams(vmem_limit_bytes=VMEM_LIMIT_BYTES),
        name="proj_in_step",
    )(x, g, b, w_f32)


def _proj_in_body(x_ref, g_ref, b_ref, w_ref, wvt_ref, *refs, apply_ln, splits, heads):
    x = x_ref[...]
    if apply_ln:
        x = _layer_norm(x, g_ref[...], b_ref[...])
        refs[0][...] = x
        refs = refs[1:]
    xb = x.astype(BF16)
    tm = x.shape[0]
    ns = len(splits)
    vt_ref, krows_ref, vrows_ref = refs[ns], refs[ns + 1], refs[ns + 2]
    off = 0
    for o_ref, width in zip(refs[:ns], splits):
        o_ref[...] = _dot(xb, w_ref[:, off:off + width])
        off += width
    vt_ref[0] = _dot_nt(wvt_ref[...], xb)
    cv = krows_ref.shape[1]
    kz = refs[ns - 1][...]
    vz = _dot(xb, w_ref[:, off:off + heads * cv])
    for h in range(heads):
        krows_ref[pl.ds(h, tm, stride=heads), :] = kz[:, h * cv:(h + 1) * cv]
        vrows_ref[pl.ds(h, tm, stride=heads), :] = vz[:, h * cv:(h + 1) * cv]


def _proj_in(x, g, b, w_bf16, splits, apply_ln, seq, heads):
    t, d = x.shape
    tm = _row_tile(t, 512)
    assert seq % tm == 0
    nt = seq // tm
    n = w_bf16.shape[1]
    wv = n - sum(splits)
    cv = wv // heads
    wvt = w_bf16[:, sum(splits):].T
    const = lambda shape: pl.BlockSpec(shape, lambda i: (0,) * len(shape))
    args = [x, g, b, w_bf16, wvt]
    in_specs = [pl.BlockSpec((tm, d), lambda i: (i, 0)), const((1, d)), const((1, d)), const((d, n)),
                const(wvt.shape)]
    out_shape = [jax.ShapeDtypeStruct((t, s), F32) for s in splits]
    out_specs = [pl.BlockSpec((tm, s), lambda i: (i, 0)) for s in splits]
    if apply_ln:
        out_shape = [jax.ShapeDtypeStruct((t, d), F32)] + out_shape
        out_specs = [pl.BlockSpec((tm, d), lambda i: (i, 0))] + out_specs
    out_shape.append(jax.ShapeDtypeStruct((t // seq, wv, seq), F32))
    out_specs.append(pl.BlockSpec((1, wv, tm), lambda i: (i // nt, 0, i % nt)))
    for _ in range(2):
        out_shape.append(jax.ShapeDtypeStruct((t * heads, cv), F32))
        out_specs.append(pl.BlockSpec((tm * heads, cv), lambda i: (i, 0)))
    return pl.pallas_call(
        functools.partial(_proj_in_body, apply_ln=apply_ln, splits=splits, heads=heads),
        grid=(t // tm,),
        in_specs=in_specs,
        out_specs=out_specs,
        out_shape=out_shape,
        compiler_params=_cparams("parallel"),
        name="proj_in",
    )(*args)


def _rglru_gates(y, wg_ref, bg_ref, lam_ref, wa):
    gm = _mm(y, wg_ref[...], wg_ref.dtype == F32) + bg_ref[...]
    r = _sigmoid(gm[:, :wa])
    i = _sigmoid(gm[:, wa:])
    log_a = LRU_C * r * _log_sigmoid(lam_ref[...])
    a = jnp.exp(log_a)
    th = jnp.tanh(log_a)
    u = jnp.sqrt(-2.0 * th / (1.0 - th)) * (i * y)
    return a, u


def _mixer_a_prompt_body(za_ref, cs_ref, h0_ref, cw_ref, cb_ref, wg_ref, bg_ref, lam_ref,
                         ya_ref, cn_ref, hn_ref, xbuf, *, tt, wa):
    t = pl.program_id(1)

    @pl.when(t == 0)
    def _():
        xbuf[5:8, :] = cs_ref[0]
        hn_ref[0] = h0_ref[0]

    x = za_ref[0, :, :wa]
    ga = za_ref[0, :, wa:]
    xbuf[8:8 + tt, :] = x
    cw = cw_ref[...]
    y = (cb_ref[...] + cw[0:1] * xbuf[5:5 + tt, :] + cw[1:2] * xbuf[6:6 + tt, :]
         + cw[2:3] * xbuf[7:7 + tt, :] + cw[3:4] * x)
    tail = x[tt - (CONV_W - 1):tt]
    xbuf[5:8, :] = tail
    cn_ref[0] = tail

    a, u = _rglru_gates(y, wg_ref, bg_ref, lam_ref, wa)
    d = 1
    while d < tt:
        a_s = _shift_rows(a, d, 1.0)
        u_s = _shift_rows(u, d, 0.0)
        u = a * u_s + u
        a = a * a_s
        d *= 2
    h = u + a * hn_ref[0]
    hn_ref[0] = h[tt - 1:tt]
    ya_ref[0] = h * _gelu_tanh(ga)


def _mixer_a_prompt(za, conv_state, h0, cw, cb, wg_bf16, bg, lam):
    bsz, length, two_wa = za.shape
    wa = two_wa // 2
    tt = _row_tile(length, 256)
    const = lambda shape: pl.BlockSpec(shape, lambda b, t: (0,) * len(shape))
    return pl.pallas_call(
        functools.partial(_mixer_a_prompt_body, tt=tt, wa=wa),
        grid=(bsz, length // tt),
        in_specs=[pl.BlockSpec((1, tt, two_wa), lambda b, t: (b, t, 0)),
                  pl.BlockSpec((1, CONV_W - 1, wa), lambda b, t: (b, 0, 0)),
                  pl.BlockSpec((1, 1, wa), lambda b, t: (b, 0, 0)),
                  const((CONV_W, wa)), const((1, wa)), const((wa, two_wa)), const((1, two_wa)),
                  const((1, wa))],
        out_specs=[pl.BlockSpec((1, tt, wa), lambda b, t: (b, t, 0)),
                   pl.BlockSpec((1, CONV_W - 1, wa), lambda b, t: (b, 0, 0)),
                   pl.BlockSpec((1, 1, wa), lambda b, t: (b, 0, 0))],
        out_shape=[jax.ShapeDtypeStruct((bsz, length, wa), F32),
                   jax.ShapeDtypeStruct((bsz, CONV_W - 1, wa), F32),
                   jax.ShapeDtypeStruct((bsz, 1, wa), F32)],
        scratch_shapes=[pltpu.VMEM((8 + tt, wa), F32)],
        compiler_params=_cparams("parallel", "arbitrary"),
        name="mixer_a_prompt",
    )(za, conv_state, h0.reshape(bsz, 1, wa), cw, cb, wg_bf16, bg, lam)


def _mixer_a_step_body(za_ref, c0_ref, c1_ref, c2_ref, h0_ref, cw_ref, cb_ref, wg_ref, bg_ref,
                       lam_ref, ya_ref, hn_ref, *, wa):
    x = za_ref[:, :wa]
    ga = za_ref[:, wa:]
    cw = cw_ref[...]
    y = (cb_ref[...] + cw[0:1] * c0_ref[...] + cw[1:2] * c1_ref[...] + cw[2:3] * c2_ref[...]
         + cw[3:4] * x)
    a, u = _rglru_gates(y, wg_ref, bg_ref, lam_ref, wa)
    h = u + a * h0_ref[...]
    hn_ref[...] = h
    ya_ref[...] = h * _gelu_tanh(ga)


def _mixer_a_step(za, conv_state, h0, cw, cb, wg_bf16, bg, lam):
    n, two_wa = za.shape
    wa = two_wa // 2
    ya, hn = pl.pallas_call(
        functools.partial(_mixer_a_step_body, wa=wa),
        out_shape=[jax.ShapeDtypeStruct((n, wa), F32), jax.ShapeDtypeStruct((n, wa), F32)],
        compiler_params=pltpu.CompilerParams(vmem_limit_bytes=VMEM_LIMIT_BYTES),
        name="mixer_a_step",
    )(za, conv_state[:, 0], conv_state[:, 1], conv_state[:, 2], h0, cw, cb, wg_bf16, bg, lam)
    conv_new = jnp.concatenate([conv_state[:, 1:], za[:, None, :wa]], axis=1)
    return ya, conv_new, hn


def _hgrn_lower_bound(hl_ref, layer):
    rows = [hl_ref[i:i + 1, :] for i in range(hl_ref.shape[0])]
    mx = functools.reduce(jnp.maximum, rows)
    es = [jnp.exp(r - mx) for r in rows]
    tot = functools.reduce(lambda p, q: p + q, es)
    ps = [e / tot for e in es]
    cum = ps[0]
    first = cum
    for i in range(1, layer + 1):
        cum = cum + ps[i]
    return cum - first


def _hgrn_inputs(z, lb, hk):
    q = _silu(z[:, 0:hk])
    fr = z[:, hk:2 * hk]
    v = z[:, 2 * hk:3 * hk]
    g = z[:, 3 * hk:4 * hk]
    log_lb = jnp.log(lb)
    c = jnp.log1p(-lb) + jnp.minimum(fr, 0.0) - _log1p_exp(-jnp.abs(fr))
    log_f = jnp.maximum(log_lb, c) + _log1p_exp(-jnp.abs(log_lb - c))
    k = (1.0 - lb) * _sigmoid(-fr)
    return q, log_f, k, v, g


def _hgrn_finish(o, g, ng_ref, bd_f32, head):
    ms = _dot(o * o, bd_f32, HIGHEST) * (1.0 / head)
    return o * lax.rsqrt(ms + RMS_EPS) * ng_ref[...] * _silu(g)


def _mixer_b_prompt_body(zb_ref, st0_ref, hl_ref, ng_ref, yb_ref, st_ref, *, layer, rows, hk, head):
    c = pl.program_id(1)

    @pl.when(c == 0)
    def _():
        st_ref[...] = st0_ref[...]

    for i in range(zb_ref.shape[0]):
        _hgrn_chunk(i, zb_ref, hl_ref, ng_ref, yb_ref, st_ref, layer=layer, rows=rows, hk=hk, head=head)


def _hgrn_chunk(i, zb_ref, hl_ref, ng_ref, yb_ref, st_ref, *, layer, rows, hk, head):
    lb = _hgrn_lower_bound(hl_ref, layer)
    q, lf, k, v, g = _hgrn_inputs(zb_ref[i], lb, hk)

    row = lax.broadcasted_iota(jnp.int32, (rows, hk), 0)
    b = lf
    d = 1
    while d < rows:
        b = b + _shift_rows(b, d, 0.0)
        d *= 2

    bd = _head_block_mask(hk, head)
    bd_f32 = jnp.where(bd, 1.0, 0.0).astype(F32)
    bd_bf16 = bd_f32.astype(BF16)
    st = st_ref[i]

    o = _dot_nt((q * jnp.exp(b)).astype(BF16), st.astype(BF16))

    nrep = hk // head
    att = jnp.zeros((rows, hk), F32)
    trow = lax.broadcasted_iota(jnp.int32, (rows, hk), 0)
    scol = lax.broadcasted_iota(jnp.int32, (rows, hk), 1) % rows
    m = rows // 2
    while m >= 8:
        span = 2 * m
        ref = jnp.concatenate(
            [jnp.broadcast_to(b[s * span + m - 1:s * span + m, :], (span, hk))
             for s in range(rows // span)], axis=0)
        second = (row // m) % 2 == 1
        qm = jnp.where(second, q * jnp.exp(b - ref), 0.0)
        km = jnp.where(second, 0.0, k * jnp.exp(ref - b)).astype(BF16)
        km_bd = jnp.where(bd, jnp.concatenate([km] * nrep, axis=0), 0.0)
        att_m = _dot_nt(qm.astype(BF16), km_bd)
        att = att + jnp.where(trow // span == scol // span, att_m, 0.0)
        m //= 2
    v_bd = jnp.where(bd, jnp.concatenate([v.astype(BF16)] * nrep, axis=0), 0.0)
    o = o + _dot(att.astype(BF16), v_bd)

    def back(x, j):
        return pltpu.roll(x.reshape(rows // 8, 8, hk), j, 1).reshape(rows, hk)

    prods = []
    for j in range(8):
        if j == 0:
            prods.append(q * k)
        else:
            valid = (row % 8) >= j
            prods.append(jnp.where(valid, q * back(k, j) * jnp.exp(b - back(b, j)), 0.0))
    w = _dot(jnp.concatenate(prods, axis=0).astype(BF16), bd_bf16)
    for j in range(8):
        v_j = v if j == 0 else back(v, j)
        o = o + w[j * rows:(j + 1) * rows] * v_j

    b_last = b[rows - 1:rows, :]
    kl = k * jnp.exp(b_last - b)
    upd = _dot_tn(v.astype(BF16), kl.astype(BF16))
    st_ref[i] = jnp.where(bd, st * jnp.exp(b_last) + upd, 0.0)

    yb_ref[i] = _hgrn_finish(o, g, ng_ref, bd_f32, head)


def _mixer_b_prompt(zb, st0, hg_lower, ng_tiled, layer):
    bsz, length, four_hk = zb.shape
    hk = four_hk // 4
    rows = 64
    assert length % rows == 0 and hk // B_HEADS == rows
    nb = next(c for c in (4, 2, 1) if bsz % c == 0)
    return pl.pallas_call(
        functools.partial(_mixer_b_prompt_body, layer=layer, rows=rows, hk=hk, head=hk // B_HEADS),
        grid=(bsz // nb, length // rows),
        in_specs=[pl.BlockSpec((nb, rows, four_hk), lambda b, c: (b, c, 0)),
                  pl.BlockSpec((nb, hk, hk), lambda b, c: (b, 0, 0)),
                  pl.BlockSpec(hg_lower.shape, lambda b, c: (0, 0)),
                  pl.BlockSpec((1, hk), lambda b, c: (0, 0))],
        out_specs=[pl.BlockSpec((nb, rows, hk), lambda b, c: (b, c, 0)),
                   pl.BlockSpec((nb, hk, hk), lambda b, c: (b, 0, 0))],
        out_shape=[jax.ShapeDtypeStruct((bsz, length, hk), F32),
                   jax.ShapeDtypeStruct((bsz, hk, hk), F32)],
        compiler_params=_cparams("parallel", "arbitrary"),
        name="mixer_b_prompt",
    )(zb, st0, hg_lower, ng_tiled)


def _mixer_b_step_body(zb_ref, vcol_ref, st0_ref, hl_ref, ng_ref, yb_ref, st_ref, *, layer, hk, head):
    lb = _hgrn_lower_bound(hl_ref, layer)
    q, lf, k, _, g = _hgrn_inputs(zb_ref[0], lb, hk)
    bd = _head_block_mask(hk, head)
    bd_f32 = jnp.where(bd, 1.0, 0.0).astype(F32)
    st = jnp.where(bd, st0_ref[0] * jnp.exp(lf) + vcol_ref[0] * k, 0.0)
    st_ref[0] = st
    q8 = jnp.broadcast_to(q, (8, hk))
    o = _dot_nt(q8, st, HIGHEST)
    g8 = jnp.broadcast_to(g, (8, hk))
    yb_ref[0] = _hgrn_finish(o, g8, ng_ref, bd_f32, head)[0:1]


def _mixer_b_step(zb, st0, hg_lower, ng_tiled, layer):
    n, four_hk = zb.shape
    hk = four_hk // 4
    vcol = zb[:, 2 * hk:3 * hk].reshape(n, hk, 1)
    yb, st = pl.pallas_call(
        functools.partial(_mixer_b_step_body, layer=layer, hk=hk, head=hk // B_HEADS),
        grid=(n,),
        in_specs=[pl.BlockSpec((1, 1, four_hk), lambda i: (i, 0, 0)),
                  pl.BlockSpec((1, hk, 1), lambda i: (i, 0, 0)),
                  pl.BlockSpec((1, hk, hk), lambda i: (i, 0, 0)),
                  pl.BlockSpec(hg_lower.shape, lambda i: (0, 0)),
                  pl.BlockSpec((1, hk), lambda i: (0, 0))],
        out_specs=[pl.BlockSpec((1, 1, hk), lambda i: (i, 0, 0)),
                   pl.BlockSpec((1, hk, hk), lambda i: (i, 0, 0))],
        out_shape=[jax.ShapeDtypeStruct((n, 1, hk), F32),
                   jax.ShapeDtypeStruct((n, hk, hk), F32)],
        compiler_params=_cparams("parallel"),
        name="mixer_b_step",
    )(zb.reshape(n, 1, four_hk), vcol, st0, hg_lower, ng_tiled)
    return yb.reshape(n, hk), st


def _state_to_block_diag(s):
    n, h, kd, vd = s.shape
    eye = jnp.eye(h, dtype=s.dtype)
    st = jnp.einsum("nhkv,hg->nhvgk", s, eye)
    return st.reshape(n, h * vd, h * kd)


def _state_from_block_diag(st, h):
    n, hv, hk = st.shape
    vd, kd = hv // h, hk // h
    st = st.reshape(n, h, vd, h, kd)
    diag = jnp.stack([st[:, i, :, i, :] for i in range(h)], axis=1)
    return jnp.swapaxes(diag, 2, 3)


def _alibi_slope(head):
    slope = 0.0
    for i in range(C_HEADS):
        slope = jnp.where(head == i, 2.0 ** (-8.0 * (i + 1) / C_HEADS), slope)
    return slope


def _diff_lambda(lq1_ref, lk1_ref, lq2_ref, lk2_ref, lam_init):
    s1 = jnp.sum(lq1_ref[...] * lk1_ref[...], axis=-1, keepdims=True)
    s2 = jnp.sum(lq2_ref[...] * lk2_ref[...], axis=-1, keepdims=True)
    return jnp.exp(s1) - jnp.exp(s2) + lam_init


def _split_bf16(x, parts):
    out = []
    rest = np.asarray(x, np.float32)
    for _ in range(parts):
        piece = rest.astype(BF16).astype(np.float32)
        out.append(piece)
        rest = rest - piece
    return out


def _alibi_tables(length, cv):
    assert length <= 64 * 256
    pos = np.arange(length)
    kx = np.zeros((length, cv), np.float32)
    kx[:, 0:3] = (pos // 64)[:, None]
    kx[:, 3:6] = (pos % 64)[:, None]
    qx = np.zeros((C_HEADS, 16, cv), np.float32)
    for h in range(C_HEADS):
        c = np.float32(2.0 ** (-8.0 * (h + 1) / C_HEADS) * LOG2E)
        pieces = _split_bf16(c, 3)
        qx[h, :, 0:3] = [64.0 * p for p in pieces]
        qx[h, :, 3:6] = pieces
    return jnp.asarray(qx, BF16), jnp.asarray(kx, BF16)


def _attn_prompt_body(qi_ref, ki_ref, q_ref, k_ref, vt_ref, qx_ref, kx_ref, lq1_ref, lk1_ref,
                      lq2_ref, lk2_ref, ngc_ref, o_ref, q2_ref, m_ref, acc_ref,
                      *, tq, tk, cv, heads, lam_init, scale):
    p = pl.program_id(1)
    qi = qi_ref[p]
    ki = ki_ref[p]
    half = cv // 2
    chains = [(h, c) for h in range(heads) for c in range(2)]

    @pl.when(ki == 0)
    def _():
        m_ref[...] = jnp.full(m_ref.shape, NEG_INF, F32)
        acc_ref[...] = jnp.zeros(acc_ref.shape, F32)
        lane = lax.broadcasted_iota(jnp.int32, (tq, cv), 1)
        for h in range(heads):
            q = q_ref[0, :, h * cv:(h + 1) * cv] * (scale * LOG2E)
            qx = jnp.broadcast_to(qx_ref[h, 0:1, :], (tq, cv))
            for c in range(2):
                qc = jnp.where((lane < half) == (c == 0), q, 0.0).astype(BF16)
                q2_ref[2 * h + c] = jnp.concatenate([qc, qx], axis=1)

    sub = lax.broadcasted_iota(jnp.int32, (16, tk), 0)
    ones = jnp.where(sub == 0, 1.0, 0.0).astype(BF16)

    keys = [jnp.concatenate([k_ref[0, :, h * cv:(h + 1) * cv].astype(BF16), kx_ref[...]], axis=1)
            for h in range(heads)]
    values = [jnp.concatenate([vt_ref[0, h * cv:(h + 1) * cv, :].astype(BF16), ones], axis=0)
              for h in range(heads)]

    def tile(masked):
        def scores(n):
            return _dot_nt(keys[chains[n][0]], q2_ref[n])

        st_next = scores(0)
        for n, (h, c) in enumerate(chains):
            st = st_next
            if n + 1 < len(chains):
                st_next = scores(n + 1)
            if masked:
                key = lax.broadcasted_iota(jnp.int32, (tk, tq), 0)
                qry = lax.broadcasted_iota(jnp.int32, (tk, tq), 1)
                st = jnp.where(key <= qry, st, NEG_INF)
            m_old = m_ref[n]
            m_new = jnp.maximum(m_old, jnp.max(st, axis=0, keepdims=True))
            pt = jnp.exp2(st - m_new).astype(BF16)
            acc_ref[n] = jnp.exp2(m_old - m_new) * acc_ref[n] + _dot(values[h], pt)
            m_ref[n] = m_new

    @pl.when(ki < qi)
    def _():
        tile(False)

    @pl.when(ki == qi)
    def _():
        tile(True)
        lam = _diff_lambda(lq1_ref, lk1_ref, lq2_ref, lk2_ref, lam_init)
        for h in range(heads):
            a1 = acc_ref[2 * h]
            a2 = acc_ref[2 * h + 1]
            ot = a1[:cv] / a1[cv:cv + 1] - lam * (a2[:cv] / a2[cv:cv + 1])
            ms = jnp.mean(ot * ot, axis=0, keepdims=True)
            ot = ot * lax.rsqrt(ms + RMS_EPS) * ngc_ref[...] * (1.0 - lam_init)
            o_ref[0, :, h * cv:(h + 1) * cv] = ot.T


def _attn_prompt(q, k, vt, lq1, lk1, lq2, lk2, ng, lam_init):
    bsz, length, width = q.shape
    cv = width // C_HEADS
    tq = tk = _row_tile(length, 512)
    nq = length // tq
    pairs = [(i, j) for i in range(nq) for j in range(i + 1)]
    qi_tab = jnp.asarray([i for i, _ in pairs], jnp.int32)
    ki_tab = jnp.asarray([j for _, j in pairs], jnp.int32)
    qx, kx = _alibi_tables(length, cv)
    const = lambda shape: pl.BlockSpec(shape, lambda b, p, qt, kt: (0,) * len(shape))
    grid_spec = pltpu.PrefetchScalarGridSpec(
        num_scalar_prefetch=2,
        grid=(bsz, len(pairs)),
        in_specs=[pl.BlockSpec((1, tq, width), lambda b, p, qt, kt: (b, qt[p], 0)),
                  pl.BlockSpec((1, tk, width), lambda b, p, qt, kt: (b, kt[p], 0)),
                  pl.BlockSpec((1, width, tk), lambda b, p, qt, kt: (b, 0, kt[p])),
                  const(qx.shape),
                  pl.BlockSpec((tk, cv), lambda b, p, qt, kt: (kt[p], 0)),
                  const(lq1.shape), const(lk1.shape), const(lq2.shape), const(lk2.shape),
                  const((cv, 1))],
        out_specs=pl.BlockSpec((1, tq, width), lambda b, p, qt, kt: (b, qt[p], 0)),
        scratch_shapes=[pltpu.VMEM((2 * C_HEADS, tq, 2 * cv), BF16),
                        pltpu.VMEM((2 * C_HEADS, 1, tq), F32),
                        pltpu.VMEM((2 * C_HEADS, cv + 16, tq), F32)],
    )
    return pl.pallas_call(
        functools.partial(_attn_prompt_body, tq=tq, tk=tk, cv=cv, heads=C_HEADS,
                          lam_init=lam_init, scale=(cv // 2) ** -0.5),
        grid_spec=grid_spec,
        out_shape=jax.ShapeDtypeStruct((bsz, length, width), F32),
        compiler_params=_cparams("parallel", "arbitrary"),
        name="attn_prompt",
    )(qi_tab, ki_tab, q, k, vt, qx, kx, lq1, lk1, lq2, lk2, ng.reshape(cv, 1))


def _attn_paged_body(pt_ref, q_ref, kn_ref, vn_ref, lq1_ref, lk1_ref, lq2_ref, lk2_ref, ng_ref,
                     *refs, pages_per_step, groups, page, cv, lam_init, scale, past):
    k_refs = refs[:pages_per_step]
    v_refs = refs[pages_per_step:2 * pages_per_step]
    o_ref, m_ref, l_ref, acc_ref = refs[2 * pages_per_step:]
    j = pl.program_id(1)
    nrow = 2 * C_HEADS
    half = cv // 2
    prow = page * C_HEADS
    gp = pages_per_step // groups

    @pl.when(j == 0)
    def _():
        m_ref[...] = jnp.full(m_ref.shape, NEG_INF, F32)
        l_ref[...] = jnp.zeros(l_ref.shape, F32)
        acc_ref[...] = jnp.zeros(acc_ref.shape, F32)

    row = lax.broadcasted_iota(jnp.int32, (nrow, 1), 0)
    lane = lax.broadcasted_iota(jnp.int32, (nrow, cv), 1)
    qbd = jnp.where((lane // half) == (row % 2), q_ref[0] * scale, 0.0)
    q_hi = qbd.astype(BF16).astype(F32)
    q_pieces = jnp.concatenate([q_hi, qbd - q_hi], axis=0).astype(BF16)
    hrow = row // 2
    slope = _alibi_slope(hrow)

    col = lax.broadcasted_iota(jnp.int32, (nrow, gp * prow), 1)
    same_head = (col % C_HEADS) == hrow
    for g in range(groups):
        pages = range(g * gp, (g + 1) * gp)
        s2 = jnp.concatenate([_dot_nt(q_pieces, k_refs[i][0, 0].astype(BF16)) for i in pages], axis=1)
        s = s2[:nrow] + s2[nrow:]
        kpos = (j * pages_per_step + g * gp) * page + col // C_HEADS
        s = jnp.where(same_head, s - slope * (past - kpos).astype(F32), NEG_INF)
        m_old = m_ref[g]
        m_new = jnp.maximum(m_old, jnp.max(s, axis=-1, keepdims=True))
        alpha = jnp.exp(m_old - m_new)
        pexp = jnp.exp(s - m_new)
        l_ref[g] = alpha * l_ref[g] + jnp.sum(pexp, axis=-1, keepdims=True)
        pb = pexp.astype(BF16)
        pv = None
        for n, i in enumerate(pages):
            term = _dot(pb[:, n * prow:(n + 1) * prow], v_refs[i][0, 0].astype(BF16))
            pv = term if pv is None else pv + term
        acc_ref[g] = alpha * acc_ref[g] + pv
        m_ref[g] = m_new

    @pl.when(j == pl.num_programs(1) - 1)
    def _():
        s_new = jnp.sum(qbd * kn_ref[0], axis=-1, keepdims=True)
        m_fin = s_new
        for g in range(groups):
            m_fin = jnp.maximum(m_fin, m_ref[g])
        p_new = jnp.exp(s_new - m_fin)
        l_fin = p_new
        acc_fin = p_new * vn_ref[0]
        for g in range(groups):
            w = jnp.exp(m_ref[g] - m_fin)
            l_fin = l_fin + w * l_ref[g]
            acc_fin = acc_fin + w * acc_ref[g]
        lam = _diff_lambda(lq1_ref, lk1_ref, lq2_ref, lk2_ref, lam_init)
        d = jnp.where(row % 2 == 0, 1.0, -lam) / l_fin * acc_fin
        o = d + pltpu.roll(d, nrow - 1, 0)
        ms = jnp.mean(o * o, axis=-1, keepdims=True)
        o_ref[0] = o * lax.rsqrt(ms + RMS_EPS) * ng_ref[...] * (1.0 - lam_init)


def _attn_paged(q, k_new, v_new, cache_k, cache_v, page_table, layer, lq1, lk1, lq2, lk2,
                ng, lam_init):
    n, width = q.shape
    cv = width // C_HEADS
    prow = cache_k.shape[2]
    page = prow // C_HEADS
    npages = page_table.shape[1]
    pps = next(c for c in (16, 8, 4, 2, 1) if npages % c == 0)
    groups = 2 if pps % 2 == 0 else 1
    nrow = 2 * C_HEADS
    const = lambda shape: pl.BlockSpec(shape, lambda b, j, pt: (0,) * len(shape))
    rows = pl.BlockSpec((1, nrow, cv), lambda b, j, pt: (b, 0, 0))
    per_half = lambda t: jnp.repeat(t.reshape(n, C_HEADS, cv), 2, axis=1)

    def page_spec(i):
        return pl.BlockSpec((1, 1, prow, cv), lambda b, j, pt: (layer, pt[b, j * pps + i], 0, 0))

    grid_spec = pltpu.PrefetchScalarGridSpec(
        num_scalar_prefetch=1,
        grid=(n, npages // pps),
        in_specs=[rows, rows, rows, const(lq1.shape), const(lk1.shape), const(lq2.shape),
                  const(lk2.shape), const(ng.shape)]
                 + [page_spec(i) for i in range(pps)] + [page_spec(i) for i in range(pps)],
        out_specs=rows,
        scratch_shapes=[pltpu.VMEM((groups, nrow, 1), F32), pltpu.VMEM((groups, nrow, 1), F32),
                        pltpu.VMEM((groups, nrow, cv), F32)],
    )
    out = pl.pallas_call(
        functools.partial(_attn_paged_body, pages_per_step=pps, groups=groups, page=page, cv=cv,
                          lam_init=lam_init, scale=(cv // 2) ** -0.5, past=npages * page),
        grid_spec=grid_spec,
        out_shape=jax.ShapeDtypeStruct((n, nrow, cv), F32),
        compiler_params=_cparams("parallel", "arbitrary"),
        name="attn_paged",
    )(page_table, per_half(q), per_half(k_new), per_half(v_new),
      lq1, lk1, lq2, lk2, ng, *([cache_k] * pps), *([cache_v] * pps))
    return out[:, 0::2, :].reshape(n, width)


def _route(logits):
    lane = lax.broadcasted_iota(jnp.int32, logits.shape, 1)
    is_g = lane < N_GROUPS
    lg = jnp.where(is_g, logits, NEG_INF)
    gmax = jnp.max(lg, axis=-1, keepdims=True)
    lane_f = lane.astype(F32)
    g_idx = jnp.min(jnp.where(lg == gmax, lane_f, float(N_GROUPS)), axis=-1, keepdims=True)
    p_top = 1.0 / jnp.sum(jnp.exp(lg - gmax), axis=-1, keepdims=True)
    lo = N_GROUPS + EXP_PER_GROUP * g_idx
    in_group = (lane_f >= lo) & (lane_f < lo + EXP_PER_GROUP)
    le = jnp.where(in_group, logits, NEG_INF)
    v1 = jnp.max(le, axis=-1, keepdims=True)
    i1 = jnp.min(jnp.where(le == v1, lane_f, 128.0), axis=-1, keepdims=True)
    le2 = jnp.where(lane_f == i1, NEG_INF, le)
    v2 = jnp.max(le2, axis=-1, keepdims=True)
    i2 = jnp.min(jnp.where(le2 == v2, lane_f, 128.0), axis=-1, keepdims=True)
    e2 = jnp.exp(v2 - v1)
    w1 = p_top / (1.0 + e2)
    w2 = p_top * e2 / (1.0 + e2)
    comb = jnp.where(lane_f == i1, w1, 0.0) + jnp.where(lane_f == i2, w2, 0.0)
    return comb[:, N_GROUPS:N_GROUPS + N_EXPERTS]


def _out_proj_body(ya_ref, yb_ref, yc_ref, x_ref, w_ref, g_ref, b_ref, wr_ref, br_ref,
                   x1_ref, comb_ref, *, alpha, wa, wb, precise):
    mixed = (_mm(ya_ref[...], w_ref[0:wa, :], precise)
             + _mm(yb_ref[...], w_ref[wa:wa + wb, :], precise)
             + _mm(yc_ref[...], w_ref[wa + wb:, :], precise))
    x1 = _layer_norm(alpha * x_ref[...] + mixed, g_ref[...], b_ref[...])
    x1_ref[...] = x1
    nl = br_ref.shape[1]
    x1h = x1.astype(BF16)
    x1l = (x1 - x1h.astype(F32)).astype(BF16)
    hh = _dot(x1h, wr_ref[...])
    logits = hh[:, :nl] + hh[:, nl:] + _dot(x1l, wr_ref[:, :nl]) + br_ref[...]
    comb_ref[...] = _route(logits)


def _out_proj(ya, yb, yc, x, w, g, b, wr, br, alpha):
    t, d = x.shape
    wa, wb, wc = ya.shape[1], yb.shape[1], yc.shape[1]
    tm = _row_tile(t, 512)
    rowspec = lambda w: pl.BlockSpec((tm, w), lambda i: (i, 0))
    const = lambda shape: pl.BlockSpec(shape, lambda i: (0,) * len(shape))
    return pl.pallas_call(
        functools.partial(_out_proj_body, alpha=alpha, wa=wa, wb=wb, precise=w.dtype == F32),
        grid=(t // tm,),
        in_specs=[rowspec(wa), rowspec(wb), rowspec(wc), rowspec(d), const(w.shape),
                  const((1, d)), const((1, d)), const(wr.shape), const(br.shape)],
        out_specs=[rowspec(d), rowspec(N_EXPERTS)],
        out_shape=[jax.ShapeDtypeStruct((t, d), F32), jax.ShapeDtypeStruct((t, N_EXPERTS), F32)],
        compiler_params=_cparams("parallel"),
        name="out_proj",
    )(ya, yb, yc, x, w, g, b, wr, br)


def _moe_body(x_ref, comb_ref, wg_ref, wu_ref, wd_ref, g_ref, b_ref, o_ref, xb_ref, acc_ref,
              *, alpha, per_step):
    s = pl.program_id(1)

    @pl.when(s == 0)
    def _():
        acc_ref[...] = jnp.zeros(acc_ref.shape, F32)
        xb_ref[...] = x_ref[...].astype(xb_ref.dtype)

    prec = HIGHEST if xb_ref.dtype == F32 else None
    xb = xb_ref[...]
    comb = comb_ref[...]
    lane = lax.broadcasted_iota(jnp.int32, comb.shape, 1)
    hids = []
    for i in range(per_step):
        ce = jnp.sum(jnp.where(lane == s * per_step + i, comb, 0.0), axis=-1, keepdims=True)
        hid = _silu(_dot(xb, wg_ref[i], prec)) * _dot(xb, wu_ref[i], prec) * ce
        hids.append(hid.astype(xb_ref.dtype))
    acc_ref[...] += _dot(jnp.concatenate(hids, axis=1), wd_ref[...], prec)

    @pl.when(s == pl.num_programs(1) - 1)
    def _():
        o_ref[...] = _layer_norm(alpha * x_ref[...] + acc_ref[...], g_ref[...], b_ref[...])


def _moe(x, comb, wg, wu, wd, g, b, alpha):
    t, d = x.shape
    ne, _, f = wg.shape
    tm = _row_tile(t, 1024)
    per_step = EXP_PER_GROUP
    return pl.pallas_call(
        functools.partial(_moe_body, alpha=alpha, per_step=per_step),
        grid=(t // tm, ne // per_step),
        in_specs=[pl.BlockSpec((tm, d), lambda i, s: (i, 0)),
                  pl.BlockSpec((tm, ne), lambda i, s: (i, 0)),
                  pl.BlockSpec((per_step, d, f), lambda i, s: (s, 0, 0)),
                  pl.BlockSpec((per_step, d, f), lambda i, s: (s, 0, 0)),
                  pl.BlockSpec((per_step * f, d), lambda i, s: (s, 0)),
                  pl.BlockSpec((1, d), lambda i, s: (0, 0)),
                  pl.BlockSpec((1, d), lambda i, s: (0, 0))],
        out_specs=pl.BlockSpec((tm, d), lambda i, s: (i, 0)),
        out_shape=jax.ShapeDtypeStruct((t, d), F32),
        scratch_shapes=[pltpu.VMEM((tm, d), wg.dtype), pltpu.VMEM((tm, d), F32)],
        compiler_params=_cparams("parallel", "arbitrary"),
        name="moe",
    )(x, comb, wg, wu, wd, g, b)


def _block_diag(w):
    n, c, _ = w.shape
    eye = jnp.eye(n, dtype=w.dtype)
    return jnp.einsum("ncd,nm->ncmd", w, eye).reshape(n * c, n * c)


def kernel(x_prompt, x_sample, cache_k, cache_v, state_conv, state_rglru, state_hgrn, page_table,
           emb_ln_g, emb_ln_b, w_in, conv_w, conv_b, rg_wr, rg_br, rg_wi, rg_bi, rg_lambda,
           hg_lower, hg_norm_g, dl_lq1, dl_lk1, dl_lq2, dl_lk2, dl_norm_g, w_out,
           ln1_g, ln1_b, ln2_g, ln2_b, moe_wg, moe_bg, moe_we, moe_be,
           ex_w_gate, ex_w_up, ex_w_down):
    depth = w_in.shape[0]
    bp, seq, d = x_prompt.shape
    bs, dseq, _ = x_sample.shape
    assert dseq == 1
    wa = conv_w.shape[-1]
    hk = hg_lower.shape[-1]
    wc = C_HEADS * dl_norm_g.shape[-1]
    splits = (2 * wa, 4 * hk, wc, wc, wc)
    alpha = (2 * depth) ** 0.25
    row = lambda v: v.reshape(1, -1)

    xp = x_prompt.reshape(bp * seq, d)
    xs = x_sample.reshape(bs, d)
    pool, page = cache_k.shape[1], cache_k.shape[2]
    ck = cache_k.reshape(depth, pool, page * C_HEADS, wc // C_HEADS)
    cvv = cache_v.reshape(depth, pool, page * C_HEADS, wc // C_HEADS)
    zeros_conv = jnp.zeros((bp, CONV_W - 1, wa), F32)
    zeros_h = jnp.zeros((bp, wa), F32)
    zeros_st = jnp.zeros((bp, hk, hk), F32)

    outs = {n: [] for n in ("conv_p", "conv_s", "lru_p", "lru_s", "hg_p", "hg_s",
                            "k_p", "k_s", "v_p", "v_s")}
    for l in range(depth):
        lam_init = 0.8 - 0.6 * math.exp(-0.3 * l)
        w_in_b = w_in[l].astype(BF16)
        w_out_b = w_out[l].astype(BF16)
        gate_w = jnp.concatenate([_block_diag(rg_wr[l]), _block_diag(rg_wi[l])], axis=1).astype(BF16)
        gate_b = jnp.concatenate([rg_br[l], rg_bi[l]]).reshape(1, -1)
        hg_ng = row(jnp.tile(hg_norm_g[l], B_HEADS))
        dl_ng = row(dl_norm_g[l])
        lq1, lk1, lq2, lk2 = row(dl_lq1[l]), row(dl_lk1[l]), row(dl_lq2[l]), row(dl_lk2[l])
        pad = jnp.zeros((d, 128 - N_GROUPS - N_EXPERTS), F32)
        wr = jnp.concatenate([moe_wg[l], moe_we[l], pad], axis=1)
        wr_hi = wr.astype(BF16)
        wr = jnp.concatenate([wr_hi, (wr - wr_hi.astype(F32)).astype(BF16)], axis=1)
        br = jnp.concatenate([moe_bg[l], moe_be[l], jnp.zeros((128 - N_GROUPS - N_EXPERTS,), F32)]).reshape(1, -1)
        wg_b, wu_b = ex_w_gate[l].astype(BF16), ex_w_up[l].astype(BF16)
        wd_b = ex_w_down[l].astype(BF16).reshape(-1, d)
        gate_w32 = jnp.concatenate([_block_diag(rg_wr[l]), _block_diag(rg_wi[l])], axis=1)
        a_params = lambda gw: (conv_w[l], row(conv_b[l]), gw, gate_b, row(rg_lambda[l]))

        def tail(ya, yb, yc, x, precise):
            wo, eg, eu, ed = ((w_out[l], ex_w_gate[l], ex_w_up[l], ex_w_down[l].reshape(-1, d))
                              if precise else (w_out_b, wg_b, wu_b, wd_b))
            x1, comb = _out_proj(ya, yb, yc, x, wo, row(ln1_g[l]), row(ln1_b[l]), wr, br, alpha)
            return _moe(x1, comb, eg, eu, ed, row(ln2_g[l]), row(ln2_b[l]), alpha)

        res = _proj_in(xp, row(emb_ln_g), row(emb_ln_b), w_in_b, splits[:4], apply_ln=(l == 0),
                       seq=seq, heads=C_HEADS)
        if l == 0:
            xp = res[0]
            res = res[1:]
        za, zb, qc, kc, vct, k_rows, v_rows = res
        ya, conv_new, h_new = _mixer_a_prompt(za.reshape(bp, seq, 2 * wa), zeros_conv, zeros_h,
                                              *a_params(gate_w))
        yb, st_new = _mixer_b_prompt(zb.reshape(bp, seq, 4 * hk), zeros_st, hg_lower, hg_ng, l)
        yc = _attn_prompt(qc.reshape(bp, seq, wc), kc.reshape(bp, seq, wc), vct,
                          lq1, lk1, lq2, lk2, dl_ng, lam_init)
        xp = tail(ya.reshape(bp * seq, wa), yb.reshape(bp * seq, hk), yc.reshape(bp * seq, wc), xp, False)
        outs["conv_p"].append(conv_new)
        outs["lru_p"].append(h_new.reshape(bp, wa))
        outs["hg_p"].append(_state_from_block_diag(st_new, B_HEADS))
        outs["k_p"].append(k_rows.reshape(bp, seq, C_HEADS, wc // C_HEADS))
        outs["v_p"].append(v_rows.reshape(bp, seq, C_HEADS, wc // C_HEADS))

        res = _proj_in_step(xs, row(emb_ln_g), row(emb_ln_b), w_in[l], splits, apply_ln=(l == 0))
        if l == 0:
            xs = res[0]
            res = res[1:]
        za, zb, qc, kc, vc = res
        ya, conv_new, h_new = _mixer_a_step(za, state_conv[l], state_rglru[l], *a_params(gate_w32))
        yb, st_new = _mixer_b_step(zb, _state_to_block_diag(state_hgrn[l]), hg_lower, hg_ng, l)
        yc = _attn_paged(qc, kc, vc, ck, cvv, page_table, l, lq1, lk1, lq2, lk2, dl_ng, lam_init)
        xs = tail(ya, yb, yc, xs, True)
        outs["conv_s"].append(conv_new)
        outs["lru_s"].append(h_new)
        outs["hg_s"].append(_state_from_block_diag(st_new, B_HEADS))
        outs["k_s"].append(kc.reshape(bs, 1, C_HEADS, wc // C_HEADS))
        outs["v_s"].append(vc.reshape(bs, 1, C_HEADS, wc // C_HEADS))

    stack = lambda n: jnp.stack(outs[n])
    return (xp.reshape(bp, seq, d), xs.reshape(bs, 1, d),
            stack("conv_p"), stack("conv_s"), stack("lru_p"), stack("lru_s"),
            stack("hg_p"), stack("hg_s"), stack("k_p"), stack("k_s"), stack("v_p"), stack("v_s"))
```

```python
import functools
import math

import jax
import jax.numpy as jnp
import numpy as np
from jax import lax
from jax.experimental import pallas as pl
from jax.experimental.pallas import tpu as pltpu

F32 = jnp.float32
BF16 = jnp.bfloat16

A_BLOCKS = 4
CONV_W = 4
LRU_C = 8.0
B_HEADS = 4
C_HEADS = 4
N_GROUPS = 4
EXP_PER_GROUP = 4
N_EXPERTS = N_GROUPS * EXP_PER_GROUP
LN_EPS = 1e-5
RMS_EPS = 1e-6
NEG_INF = float("-inf")
LOG2E = math.log2(math.e)

VMEM_LIMIT_BYTES = 56 * 1024 * 1024
LANES = 128
PROJ_ROWS = 512
MOE_ROWS = 1024
SCAN_ROWS = 256
HGRN_CHUNK = 64
HGRN_SEQS = (8, 4, 2, 1)
ATTN_TILE = 512
PAGES_PER_STEP = (16, 8, 4, 2, 1)
SCORES_AHEAD = 3
HIGHEST = lax.Precision.HIGHEST


def _cparams(*sem):
    return pltpu.CompilerParams(dimension_semantics=sem, vmem_limit_bytes=VMEM_LIMIT_BYTES)


def _row_tile(n, want):
    return want if n % want == 0 else n


def _layer_norm(x, g, b):
    mu = jnp.mean(x, axis=-1, keepdims=True)
    xc = x - mu
    var = jnp.mean(xc * xc, axis=-1, keepdims=True)
    return xc * lax.rsqrt(var + LN_EPS) * g + b


def _sigmoid(x):
    return 0.5 * jnp.tanh(0.5 * x) + 0.5


def _silu(x):
    return x * _sigmoid(x)


def _log_sigmoid(x):
    return jnp.minimum(x, 0.0) - jnp.log1p(jnp.exp(-jnp.abs(x)))


def _log1p_exp(x):
    return jnp.log(1.0 + jnp.exp(x))


def _gelu_tanh(x):
    c = math.sqrt(2.0 / math.pi)
    return 0.5 * x * (1.0 + jnp.tanh(c * (x + 0.044715 * (x * x * x))))


def _dot(a, b, precision=None):
    return jnp.dot(a, b, preferred_element_type=F32, precision=precision)


def _dot_nt(a, b, precision=None):
    return lax.dot_general(a, b, (((1,), (1,)), ((), ())), preferred_element_type=F32,
                           precision=precision)


def _dot_tn(a, b, precision=None):
    return lax.dot_general(a, b, (((0,), (0,)), ((), ())), preferred_element_type=F32,
                           precision=precision)


def _shift_rows(x, d, fill):
    row = lax.broadcasted_iota(jnp.int32, x.shape, 0)
    return jnp.where(row >= d, pltpu.roll(x, d, 0), fill)


def _head_block_mask(n, head):
    r = lax.broadcasted_iota(jnp.int32, (n, n), 0) // head
    c = lax.broadcasted_iota(jnp.int32, (n, n), 1) // head
    return r == c


def _split_hi_lo(x):
    bits = lax.bitcast_convert_type(x, jnp.uint32) & jnp.uint32(0xFFFF0000)
    hi = lax.bitcast_convert_type(bits, F32)
    return hi, x - hi


def _mm(a, w, precise):
    if precise:
        return _dot(a.astype(F32), w, HIGHEST)
    return _dot(a.astype(BF16), w)


def _proj_in_step_body(x_ref, g_ref, b_ref, w_ref, *out_refs, apply_ln, splits):
    x = x_ref[...]
    if apply_ln:
        x = _layer_norm(x, g_ref[...], b_ref[...])
        out_refs[0][...] = x
        out_refs = out_refs[1:]
    off = 0
    for o_ref, width in zip(out_refs, splits):
        o_ref[...] = _mm(x, w_ref[:, off:off + width], True)
        off += width


def _proj_in_step(x, g, b, w_f32, splits, apply_ln):
    t, d = x.shape
    out_shape = [jax.ShapeDtypeStruct((t, s), F32) for s in splits]
    if apply_ln:
        out_shape = [jax.ShapeDtypeStruct((t, d), F32)] + out_shape
    return pl.pallas_call(
        functools.partial(_proj_in_step_body, apply_ln=apply_ln, splits=splits),
        out_shape=out_shape,
        compiler_params=pltpu.CompilerParams(vmem_limit_bytes=VMEM_LIMIT_BYTES),
        name="proj_in_step",
    )(x, g, b, w_f32)


def _proj_in_body(x_ref, g_ref, b_ref, w_ref, wvt_ref, *refs, apply_ln, splits, heads):
    x = x_ref[...]
    if apply_ln:
        x = _layer_norm(x, g_ref[...], b_ref[...])
        refs[0][...] = x
        refs = refs[1:]
    xb = x.astype(BF16)
    tm = x.shape[0]
    ns = len(splits)
    vt_ref, krows_ref, vrows_ref = refs[ns], refs[ns + 1], refs[ns + 2]
    off = 0
    for o_ref, width in zip(refs[:ns], splits):
        o_ref[...] = _dot(xb, w_ref[:, off:off + width])
        off += width
    vt_ref[0] = _dot_nt(wvt_ref[...], xb)
    cv = krows_ref.shape[1]
    kz = refs[ns - 1][...]
    vz = _dot(xb, w_ref[:, off:off + heads * cv])
    for h in range(heads):
        krows_ref[pl.ds(h, tm, stride=heads), :] = kz[:, h * cv:(h + 1) * cv]
        vrows_ref[pl.ds(h, tm, stride=heads), :] = vz[:, h * cv:(h + 1) * cv]


def _proj_in(x, g, b, w_bf16, splits, apply_ln, seq, heads):
    t, d = x.shape
    tm = _row_tile(t, PROJ_ROWS)
    assert seq % tm == 0
    nt = seq // tm
    n = w_bf16.shape[1]
    wv = n - sum(splits)
    cv = wv // heads
    wvt = w_bf16[:, sum(splits):].T
    const = lambda shape: pl.BlockSpec(shape, lambda i: (0,) * len(shape))
    args = [x, g, b, w_bf16, wvt]
    in_specs = [pl.BlockSpec((tm, d), lambda i: (i, 0)), const((1, d)), const((1, d)), const((d, n)),
                const(wvt.shape)]
    out_shape = [jax.ShapeDtypeStruct((t, s), F32) for s in splits]
    out_specs = [pl.BlockSpec((tm, s), lambda i: (i, 0)) for s in splits]
    if apply_ln:
        out_shape = [jax.ShapeDtypeStruct((t, d), F32)] + out_shape
        out_specs = [pl.BlockSpec((tm, d), lambda i: (i, 0))] + out_specs
    out_shape.append(jax.ShapeDtypeStruct((t // seq, wv, seq), F32))
    out_specs.append(pl.BlockSpec((1, wv, tm), lambda i: (i // nt, 0, i % nt)))
    for _ in range(2):
        out_shape.append(jax.ShapeDtypeStruct((t * heads, cv), F32))
        out_specs.append(pl.BlockSpec((tm * heads, cv), lambda i: (i, 0)))
    return pl.pallas_call(
        functools.partial(_proj_in_body, apply_ln=apply_ln, splits=splits, heads=heads),
        grid=(t // tm,),
        in_specs=in_specs,
        out_specs=out_specs,
        out_shape=out_shape,
        compiler_params=_cparams("parallel"),
        name="proj_in",
    )(*args)


def _rglru_gates(y, wg_ref, bg_ref, lam_ref, wa):
    gm = _mm(y, wg_ref[...], wg_ref.dtype == F32) + bg_ref[...]
    r = _sigmoid(gm[:, :wa])
    i = _sigmoid(gm[:, wa:])
    log_a = LRU_C * r * _log_sigmoid(lam_ref[...])
    a = jnp.exp(log_a)
    th = jnp.tanh(log_a)
    u = jnp.sqrt(-2.0 * th / (1.0 - th)) * (i * y)
    return a, u


def _mixer_a_prompt_body(za_ref, cs_ref, h0_ref, cw_ref, cb_ref, wg_ref, bg_ref, lam_ref,
                         ya_ref, cn_ref, hn_ref, xbuf, *, tt, wa):
    t = pl.program_id(1)

    @pl.when(t == 0)
    def _():
        xbuf[5:8, :] = cs_ref[0]
        hn_ref[0] = h0_ref[0]

    x = za_ref[0, :, :wa]
    ga = za_ref[0, :, wa:]
    xbuf[8:8 + tt, :] = x
    cw = cw_ref[...]
    y = (cb_ref[...] + cw[0:1] * xbuf[5:5 + tt, :] + cw[1:2] * xbuf[6:6 + tt, :]
         + cw[2:3] * xbuf[7:7 + tt, :] + cw[3:4] * x)
    tail = x[tt - (CONV_W - 1):tt]
    xbuf[5:8, :] = tail
    cn_ref[0] = tail

    a, u = _rglru_gates(y, wg_ref, bg_ref, lam_ref, wa)
    d = 1
    while d < tt:
        a_s = _shift_rows(a, d, 1.0)
        u_s = _shift_rows(u, d, 0.0)
        u = a * u_s + u
        a = a * a_s
        d *= 2
    h = u + a * hn_ref[0]
    hn_ref[0] = h[tt - 1:tt]
    ya_ref[0] = h * _gelu_tanh(ga)


def _mixer_a_prompt(za, conv_state, h0, cw, cb, wg_bf16, bg, lam):
    bsz, length, two_wa = za.shape
    wa = two_wa // 2
    tt = _row_tile(length, SCAN_ROWS)
    const = lambda shape: pl.BlockSpec(shape, lambda b, t: (0,) * len(shape))
    return pl.pallas_call(
        functools.partial(_mixer_a_prompt_body, tt=tt, wa=wa),
        grid=(bsz, length // tt),
        in_specs=[pl.BlockSpec((1, tt, two_wa), lambda b, t: (b, t, 0)),
                  pl.BlockSpec((1, CONV_W - 1, wa), lambda b, t: (b, 0, 0)),
                  pl.BlockSpec((1, 1, wa), lambda b, t: (b, 0, 0)),
                  const((CONV_W, wa)), const((1, wa)), const((wa, two_wa)), const((1, two_wa)),
                  const((1, wa))],
        out_specs=[pl.BlockSpec((1, tt, wa), lambda b, t: (b, t, 0)),
                   pl.BlockSpec((1, CONV_W - 1, wa), lambda b, t: (b, 0, 0)),
                   pl.BlockSpec((1, 1, wa), lambda b, t: (b, 0, 0))],
        out_shape=[jax.ShapeDtypeStruct((bsz, length, wa), F32),
                   jax.ShapeDtypeStruct((bsz, CONV_W - 1, wa), F32),
                   jax.ShapeDtypeStruct((bsz, 1, wa), F32)],
        scratch_shapes=[pltpu.VMEM((8 + tt, wa), F32)],
        compiler_params=_cparams("parallel", "arbitrary"),
        name="mixer_a_prompt",
    )(za, conv_state, h0.reshape(bsz, 1, wa), cw, cb, wg_bf16, bg, lam)


def _mixer_a_step_body(za_ref, c0_ref, c1_ref, c2_ref, h0_ref, cw_ref, cb_ref, wg_ref, bg_ref,
                       lam_ref, ya_ref, hn_ref, *, wa):
    x = za_ref[:, :wa]
    ga = za_ref[:, wa:]
    cw = cw_ref[...]
    y = (cb_ref[...] + cw[0:1] * c0_ref[...] + cw[1:2] * c1_ref[...] + cw[2:3] * c2_ref[...]
         + cw[3:4] * x)
    a, u = _rglru_gates(y, wg_ref, bg_ref, lam_ref, wa)
    h = u + a * h0_ref[...]
    hn_ref[...] = h
    ya_ref[...] = h * _gelu_tanh(ga)


def _mixer_a_step(za, conv_state, h0, cw, cb, wg_bf16, bg, lam):
    n, two_wa = za.shape
    wa = two_wa // 2
    ya, hn = pl.pallas_call(
        functools.partial(_mixer_a_step_body, wa=wa),
        out_shape=[jax.ShapeDtypeStruct((n, wa), F32), jax.ShapeDtypeStruct((n, wa), F32)],
        compiler_params=pltpu.CompilerParams(vmem_limit_bytes=VMEM_LIMIT_BYTES),
        name="mixer_a_step",
    )(za, conv_state[:, 0], conv_state[:, 1], conv_state[:, 2], h0, cw, cb, wg_bf16, bg, lam)
    conv_new = jnp.concatenate([conv_state[:, 1:], za[:, None, :wa]], axis=1)
    return ya, conv_new, hn


def _hgrn_lower_bound(hl_ref, layer):
    rows = [hl_ref[i:i + 1, :] for i in range(hl_ref.shape[0])]
    mx = functools.reduce(jnp.maximum, rows)
    es = [jnp.exp(r - mx) for r in rows]
    tot = functools.reduce(lambda p, q: p + q, es)
    ps = [e / tot for e in es]
    cum = ps[0]
    first = cum
    for i in range(1, layer + 1):
        cum = cum + ps[i]
    return cum - first


def _hgrn_inputs(z, lb, hk):
    q = _silu(z[:, 0:hk])
    fr = z[:, hk:2 * hk]
    v = z[:, 2 * hk:3 * hk]
    g = z[:, 3 * hk:4 * hk]
    log_lb = jnp.log(lb)
    c = jnp.log1p(-lb) + jnp.minimum(fr, 0.0) - _log1p_exp(-jnp.abs(fr))
    log_f = jnp.maximum(log_lb, c) + _log1p_exp(-jnp.abs(log_lb - c))
    k = (1.0 - lb) * _sigmoid(-fr)
    return q, log_f, k, v, g


def _hgrn_finish(o, g, ng_ref, bd_f32, head):
    ms = _dot(o * o, bd_f32, HIGHEST) * (1.0 / head)
    return o * lax.rsqrt(ms + RMS_EPS) * ng_ref[...] * _silu(g)


def _mixer_b_prompt_body(zb_ref, st0_ref, hl_ref, ng_ref, yb_ref, st_ref, *, layer, rows, hk, head):
    c = pl.program_id(1)

    @pl.when(c == 0)
    def _():
        st_ref[...] = st0_ref[...]

    for i in range(zb_ref.shape[0]):
        _hgrn_chunk(i, zb_ref, hl_ref, ng_ref, yb_ref, st_ref, layer=layer, rows=rows, hk=hk, head=head)


def _hgrn_chunk(i, zb_ref, hl_ref, ng_ref, yb_ref, st_ref, *, layer, rows, hk, head):
    lb = _hgrn_lower_bound(hl_ref, layer)
    q, lf, k, v, g = _hgrn_inputs(zb_ref[i], lb, hk)

    row = lax.broadcasted_iota(jnp.int32, (rows, hk), 0)
    b = lf
    d = 1
    while d < rows:
        b = b + _shift_rows(b, d, 0.0)
        d *= 2

    bd = _head_block_mask(hk, head)
    bd_f32 = jnp.where(bd, 1.0, 0.0).astype(F32)
    bd_bf16 = bd_f32.astype(BF16)
    st = st_ref[i]

    o = _dot_nt((q * jnp.exp(b)).astype(BF16), st.astype(BF16))

    nrep = hk // head
    att = jnp.zeros((rows, hk), F32)
    trow = lax.broadcasted_iota(jnp.int32, (rows, hk), 0)
    scol = lax.broadcasted_iota(jnp.int32, (rows, hk), 1) % rows
    m = rows // 2
    while m >= 8:
        span = 2 * m
        ref = jnp.concatenate(
            [jnp.broadcast_to(b[s * span + m - 1:s * span + m, :], (span, hk))
             for s in range(rows // span)], axis=0)
        second = (row // m) % 2 == 1
        qm = jnp.where(second, q * jnp.exp(b - ref), 0.0)
        km = jnp.where(second, 0.0, k * jnp.exp(ref - b)).astype(BF16)
        km_bd = jnp.where(bd, jnp.concatenate([km] * nrep, axis=0), 0.0)
        att_m = _dot_nt(qm.astype(BF16), km_bd)
        att = att + jnp.where(trow // span == scol // span, att_m, 0.0)
        m //= 2
    v_bd = jnp.where(bd, jnp.concatenate([v.astype(BF16)] * nrep, axis=0), 0.0)
    o = o + _dot(att.astype(BF16), v_bd)

    def back(x, j):
        return pltpu.roll(x.reshape(rows // 8, 8, hk), j, 1).reshape(rows, hk)

    prods = []
    for j in range(8):
        if j == 0:
            prods.append(q * k)
        else:
            valid = (row % 8) >= j
            prods.append(jnp.where(valid, q * back(k, j) * jnp.exp(b - back(b, j)), 0.0))
    w = _dot(jnp.concatenate(prods, axis=0).astype(BF16), bd_bf16)
    for j in range(8):
        v_j = v if j == 0 else back(v, j)
        o = o + w[j * rows:(j + 1) * rows] * v_j

    b_last = b[rows - 1:rows, :]
    kl = k * jnp.exp(b_last - b)
    upd = _dot_tn(v.astype(BF16), kl.astype(BF16))
    st_ref[i] = jnp.where(bd, st * jnp.exp(b_last) + upd, 0.0)

    yb_ref[i] = _hgrn_finish(o, g, ng_ref, bd_f32, head)


def _mixer_b_prompt(zb, st0, hg_lower, ng_tiled, layer):
    bsz, length, four_hk = zb.shape
    hk = four_hk // 4
    rows = HGRN_CHUNK
    assert length % rows == 0 and hk // B_HEADS == rows
    nb = next(c for c in HGRN_SEQS if bsz % c == 0)
    return pl.pallas_call(
        functools.partial(_mixer_b_prompt_body, layer=layer, rows=rows, hk=hk, head=hk // B_HEADS),
        grid=(bsz // nb, length // rows),
        in_specs=[pl.BlockSpec((nb, rows, four_hk), lambda b, c: (b, c, 0)),
                  pl.BlockSpec((nb, hk, hk), lambda b, c: (b, 0, 0)),
                  pl.BlockSpec(hg_lower.shape, lambda b, c: (0, 0)),
                  pl.BlockSpec((1, hk), lambda b, c: (0, 0))],
        out_specs=[pl.BlockSpec((nb, rows, hk), lambda b, c: (b, c, 0)),
                   pl.BlockSpec((nb, hk, hk), lambda b, c: (b, 0, 0))],
        out_shape=[jax.ShapeDtypeStruct((bsz, length, hk), F32),
                   jax.ShapeDtypeStruct((bsz, hk, hk), F32)],
        compiler_params=_cparams("parallel", "arbitrary"),
        name="mixer_b_prompt",
    )(zb, st0, hg_lower, ng_tiled)


def _mixer_b_step_body(zb_ref, vcol_ref, st0_ref, hl_ref, ng_ref, yb_ref, st_ref, *, layer, hk, head):
    lb = _hgrn_lower_bound(hl_ref, layer)
    q, lf, k, _, g = _hgrn_inputs(zb_ref[0], lb, hk)
    bd = _head_block_mask(hk, head)
    bd_f32 = jnp.where(bd, 1.0, 0.0).astype(F32)
    st = jnp.where(bd, st0_ref[0] * jnp.exp(lf) + vcol_ref[0] * k, 0.0)
    st_ref[0] = st
    q8 = jnp.broadcast_to(q, (8, hk))
    o = _dot_nt(q8, st, HIGHEST)
    g8 = jnp.broadcast_to(g, (8, hk))
    yb_ref[0] = _hgrn_finish(o, g8, ng_ref, bd_f32, head)[0:1]


def _mixer_b_step(zb, st0, hg_lower, ng_tiled, layer):
    n, four_hk = zb.shape
    hk = four_hk // 4
    vcol = zb[:, 2 * hk:3 * hk].reshape(n, hk, 1)
    yb, st = pl.pallas_call(
        functools.partial(_mixer_b_step_body, layer=layer, hk=hk, head=hk // B_HEADS),
        grid=(n,),
        in_specs=[pl.BlockSpec((1, 1, four_hk), lambda i: (i, 0, 0)),
                  pl.BlockSpec((1, hk, 1), lambda i: (i, 0, 0)),
                  pl.BlockSpec((1, hk, hk), lambda i: (i, 0, 0)),
                  pl.BlockSpec(hg_lower.shape, lambda i: (0, 0)),
                  pl.BlockSpec((1, hk), lambda i: (0, 0))],
        out_specs=[pl.BlockSpec((1, 1, hk), lambda i: (i, 0, 0)),
                   pl.BlockSpec((1, hk, hk), lambda i: (i, 0, 0))],
        out_shape=[jax.ShapeDtypeStruct((n, 1, hk), F32),
                   jax.ShapeDtypeStruct((n, hk, hk), F32)],
        compiler_params=_cparams("parallel"),
        name="mixer_b_step",
    )(zb.reshape(n, 1, four_hk), vcol, st0, hg_lower, ng_tiled)
    return yb.reshape(n, hk), st


def _state_to_block_diag(s):
    n, h, kd, vd = s.shape
    eye = jnp.eye(h, dtype=s.dtype)
    st = jnp.einsum("nhkv,hg->nhvgk", s, eye)
    return st.reshape(n, h * vd, h * kd)


def _state_from_block_diag(st, h):
    n, hv, hk = st.shape
    vd, kd = hv // h, hk // h
    st = st.reshape(n, h, vd, h, kd)
    diag = jnp.stack([st[:, i, :, i, :] for i in range(h)], axis=1)
    return jnp.swapaxes(diag, 2, 3)


def _alibi_slope(head):
    slope = 0.0
    for i in range(C_HEADS):
        slope = jnp.where(head == i, 2.0 ** (-8.0 * (i + 1) / C_HEADS), slope)
    return slope


def _diff_lambda(lq1_ref, lk1_ref, lq2_ref, lk2_ref, lam_init):
    s1 = jnp.sum(lq1_ref[...] * lk1_ref[...], axis=-1, keepdims=True)
    s2 = jnp.sum(lq2_ref[...] * lk2_ref[...], axis=-1, keepdims=True)
    return jnp.exp(s1) - jnp.exp(s2) + lam_init


def _split_bf16(x, parts):
    out = []
    rest = np.asarray(x, np.float32)
    for _ in range(parts):
        piece = rest.astype(BF16).astype(np.float32)
        out.append(piece)
        rest = rest - piece
    return out


def _alibi_tables(length, cv):
    assert length <= 64 * 256
    pos = np.arange(length)
    kx = np.zeros((length, cv), np.float32)
    kx[:, 0:3] = (pos // 64)[:, None]
    kx[:, 3:6] = (pos % 64)[:, None]
    qx = np.zeros((C_HEADS, 16, cv), np.float32)
    for h in range(C_HEADS):
        c = np.float32(2.0 ** (-8.0 * (h + 1) / C_HEADS) * LOG2E)
        pieces = _split_bf16(c, 3)
        qx[h, :, 0:3] = [64.0 * p for p in pieces]
        qx[h, :, 3:6] = pieces
    return jnp.asarray(qx, BF16), jnp.asarray(kx, BF16)


def _attn_prompt_body(qi_ref, ki_ref, q_ref, k_ref, vt_ref, qx_ref, kx_ref, lq1_ref, lk1_ref,
                      lq2_ref, lk2_ref, ngc_ref, o_ref, q2_ref, m_ref, acc_ref,
                      *, tq, tk, cv, heads, lam_init, scale):
    p = pl.program_id(1)
    qi = qi_ref[p]
    ki = ki_ref[p]
    half = cv // 2
    chains = [(h, c) for h in range(heads) for c in range(2)]

    @pl.when(ki == 0)
    def _():
        m_ref[...] = jnp.full(m_ref.shape, NEG_INF, F32)
        acc_ref[...] = jnp.zeros(acc_ref.shape, F32)
        lane = lax.broadcasted_iota(jnp.int32, (tq, cv), 1)
        for h in range(heads):
            q = q_ref[0, :, h * cv:(h + 1) * cv] * (scale * LOG2E)
            qx = jnp.broadcast_to(qx_ref[h, 0:1, :], (tq, cv))
            for c in range(2):
                qc = jnp.where((lane < half) == (c == 0), q, 0.0).astype(BF16)
                q2_ref[2 * h + c] = jnp.concatenate([qc, qx], axis=1)

    sub = lax.broadcasted_iota(jnp.int32, (16, tk), 0)
    ones = jnp.where(sub == 0, 1.0, 0.0).astype(BF16)

    keys = [jnp.concatenate([k_ref[0, :, h * cv:(h + 1) * cv].astype(BF16), kx_ref[...]], axis=1)
            for h in range(heads)]
    values = [jnp.concatenate([vt_ref[0, h * cv:(h + 1) * cv, :].astype(BF16), ones], axis=0)
              for h in range(heads)]

    def tile(masked):
        def scores(n):
            return _dot_nt(keys[chains[n][0]], q2_ref[n])

        ahead = SCORES_AHEAD
        queue = [scores(n) for n in range(min(ahead, len(chains)))]
        for n, (h, c) in enumerate(chains):
            st = queue.pop(0)
            if n + ahead < len(chains):
                queue.append(scores(n + ahead))
            if masked:
                key = lax.broadcasted_iota(jnp.int32, (tk, tq), 0)
                qry = lax.broadcasted_iota(jnp.int32, (tk, tq), 1)
                st = jnp.where(key <= qry, st, NEG_INF)
            m_old = m_ref[n]
            m_new = jnp.maximum(m_old, jnp.max(st, axis=0, keepdims=True))
            pt = jnp.exp2(st - m_new).astype(BF16)
            acc_ref[n] = jnp.exp2(m_old - m_new) * acc_ref[n] + _dot(values[h], pt)
            m_ref[n] = m_new

    @pl.when(ki < qi)
    def _():
        tile(False)

    @pl.when(ki == qi)
    def _():
        tile(True)
        lam = _diff_lambda(lq1_ref, lk1_ref, lq2_ref, lk2_ref, lam_init)
        for h in range(heads):
            a1 = acc_ref[2 * h]
            a2 = acc_ref[2 * h + 1]
            ot = a1[:cv] / a1[cv:cv + 1] - lam * (a2[:cv] / a2[cv:cv + 1])
            ms = jnp.mean(ot * ot, axis=0, keepdims=True)
            ot = ot * lax.rsqrt(ms + RMS_EPS) * ngc_ref[...] * (1.0 - lam_init)
            o_ref[0, :, h * cv:(h + 1) * cv] = ot.T


def _attn_prompt(q, k, vt, lq1, lk1, lq2, lk2, ng, lam_init):
    bsz, length, width = q.shape
    cv = width // C_HEADS
    tq = tk = _row_tile(length, ATTN_TILE)
    nq = length // tq
    pairs = [(i, j) for i in range(nq) for j in range(i + 1)]
    qi_tab = jnp.asarray([i for i, _ in pairs], jnp.int32)
    ki_tab = jnp.asarray([j for _, j in pairs], jnp.int32)
    qx, kx = _alibi_tables(length, cv)
    const = lambda shape: pl.BlockSpec(shape, lambda b, p, qt, kt: (0,) * len(shape))
    grid_spec = pltpu.PrefetchScalarGridSpec(
        num_scalar_prefetch=2,
        grid=(bsz, len(pairs)),
        in_specs=[pl.BlockSpec((1, tq, width), lambda b, p, qt, kt: (b, qt[p], 0)),
                  pl.BlockSpec((1, tk, width), lambda b, p, qt, kt: (b, kt[p], 0)),
                  pl.BlockSpec((1, width, tk), lambda b, p, qt, kt: (b, 0, kt[p])),
                  const(qx.shape),
                  pl.BlockSpec((tk, cv), lambda b, p, qt, kt: (kt[p], 0)),
                  const(lq1.shape), const(lk1.shape), const(lq2.shape), const(lk2.shape),
                  const((cv, 1))],
        out_specs=pl.BlockSpec((1, tq, width), lambda b, p, qt, kt: (b, qt[p], 0)),
        scratch_shapes=[pltpu.VMEM((2 * C_HEADS, tq, 2 * cv), BF16),
                        pltpu.VMEM((2 * C_HEADS, 1, tq), F32),
                        pltpu.VMEM((2 * C_HEADS, cv + 16, tq), F32)],
    )
    return pl.pallas_call(
        functools.partial(_attn_prompt_body, tq=tq, tk=tk, cv=cv, heads=C_HEADS,
                          lam_init=lam_init, scale=(cv // 2) ** -0.5),
        grid_spec=grid_spec,
        out_shape=jax.ShapeDtypeStruct((bsz, length, width), F32),
        compiler_params=_cparams("parallel", "arbitrary"),
        name="attn_prompt",
    )(qi_tab, ki_tab, q, k, vt, qx, kx, lq1, lk1, lq2, lk2, ng.reshape(cv, 1))


def _attn_paged_body(pt_ref, q_ref, kn_ref, vn_ref, lq1_ref, lk1_ref, lq2_ref, lk2_ref, ng_ref,
                     *refs, pages_per_step, groups, page, cv, lam_init, scale, past):
    k_refs = refs[:pages_per_step]
    v_refs = refs[pages_per_step:2 * pages_per_step]
    o_ref, m_ref, l_ref, acc_ref = refs[2 * pages_per_step:]
    j = pl.program_id(1)
    nrow = 2 * C_HEADS
    half = cv // 2
    prow = page * C_HEADS
    gp = pages_per_step // groups

    @pl.when(j == 0)
    def _():
        m_ref[...] = jnp.full(m_ref.shape, NEG_INF, F32)
        l_ref[...] = jnp.zeros(l_ref.shape, F32)
        acc_ref[...] = jnp.zeros(acc_ref.shape, F32)

    row = lax.broadcasted_iota(jnp.int32, (nrow, 1), 0)
    lane = lax.broadcasted_iota(jnp.int32, (nrow, cv), 1)
    qbd = jnp.where((lane // half) == (row % 2), q_ref[0] * scale, 0.0)
    q_pieces = jnp.concatenate(_split_hi_lo(qbd), axis=0).astype(BF16)
    hrow = row // 2
    slope = _alibi_slope(hrow)

    col = lax.broadcasted_iota(jnp.int32, (nrow, gp * prow), 1)
    same_head = (col % C_HEADS) == hrow
    for g in range(groups):
        pages = range(g * gp, (g + 1) * gp)
        s2 = jnp.concatenate([_dot_nt(q_pieces, k_refs[i][0, 0].astype(BF16)) for i in pages], axis=1)
        s = s2[:nrow] + s2[nrow:]
        kpos = (j * pages_per_step + g * gp) * page + col // C_HEADS
        s = jnp.where(same_head, s - slope * (past - kpos).astype(F32), NEG_INF)
        m_old = m_ref[g]
        m_new = jnp.maximum(m_old, jnp.max(s, axis=-1, keepdims=True))
        alpha = jnp.exp(m_old - m_new)
        pexp = jnp.exp(s - m_new)
        l_ref[g] = alpha * l_ref[g] + jnp.sum(pexp, axis=-1, keepdims=True)
        pb = pexp.astype(BF16)
        pv = None
        for n, i in enumerate(pages):
            term = _dot(pb[:, n * prow:(n + 1) * prow], v_refs[i][0, 0].astype(BF16))
            pv = term if pv is None else pv + term
        acc_ref[g] = alpha * acc_ref[g] + pv
        m_ref[g] = m_new

    @pl.when(j == pl.num_programs(1) - 1)
    def _():
        s_new = jnp.sum(qbd * kn_ref[0], axis=-1, keepdims=True)
        m_fin = s_new
        for g in range(groups):
            m_fin = jnp.maximum(m_fin, m_ref[g])
        p_new = jnp.exp(s_new - m_fin)
        l_fin = p_new
        acc_fin = p_new * vn_ref[0]
        for g in range(groups):
            w = jnp.exp(m_ref[g] - m_fin)
            l_fin = l_fin + w * l_ref[g]
            acc_fin = acc_fin + w * acc_ref[g]
        lam = _diff_lambda(lq1_ref, lk1_ref, lq2_ref, lk2_ref, lam_init)
        d = jnp.where(row % 2 == 0, 1.0, -lam) / l_fin * acc_fin
        o = d + pltpu.roll(d, nrow - 1, 0)
        ms = jnp.mean(o * o, axis=-1, keepdims=True)
        o_ref[0] = o * lax.rsqrt(ms + RMS_EPS) * ng_ref[...] * (1.0 - lam_init)


def _attn_paged(q, k_new, v_new, cache_k, cache_v, page_table, layer, lq1, lk1, lq2, lk2,
                ng, lam_init):
    n, width = q.shape
    cv = width // C_HEADS
    prow = cache_k.shape[2]
    page = prow // C_HEADS
    npages = page_table.shape[1]
    pps = next(c for c in PAGES_PER_STEP if npages % c == 0)
    groups = 2 if pps % 2 == 0 else 1
    nrow = 2 * C_HEADS
    const = lambda shape: pl.BlockSpec(shape, lambda b, j, pt: (0,) * len(shape))
    rows = pl.BlockSpec((1, nrow, cv), lambda b, j, pt: (b, 0, 0))
    per_half = lambda t: jnp.repeat(t.reshape(n, C_HEADS, cv), 2, axis=1)

    def page_spec(i):
        return pl.BlockSpec((1, 1, prow, cv), lambda b, j, pt: (layer, pt[b, j * pps + i], 0, 0))

    grid_spec = pltpu.PrefetchScalarGridSpec(
        num_scalar_prefetch=1,
        grid=(n, npages // pps),
        in_specs=[rows, rows, rows, const(lq1.shape), const(lk1.shape), const(lq2.shape),
                  const(lk2.shape), const(ng.shape)]
                 + [page_spec(i) for i in range(pps)] + [page_spec(i) for i in range(pps)],
        out_specs=rows,
        scratch_shapes=[pltpu.VMEM((groups, nrow, 1), F32), pltpu.VMEM((groups, nrow, 1), F32),
                        pltpu.VMEM((groups, nrow, cv), F32)],
    )
    out = pl.pallas_call(
        functools.partial(_attn_paged_body, pages_per_step=pps, groups=groups, page=page, cv=cv,
                          lam_init=lam_init, scale=(cv // 2) ** -0.5, past=npages * page),
        grid_spec=grid_spec,
        out_shape=jax.ShapeDtypeStruct((n, nrow, cv), F32),
        compiler_params=_cparams("parallel", "arbitrary"),
        name="attn_paged",
    )(page_table, per_half(q), per_half(k_new), per_half(v_new),
      lq1, lk1, lq2, lk2, ng, *([cache_k] * pps), *([cache_v] * pps))
    return out[:, 0::2, :].reshape(n, width)


def _route(logits):
    lane = lax.broadcasted_iota(jnp.int32, logits.shape, 1)
    is_g = lane < N_GROUPS
    lg = jnp.where(is_g, logits, NEG_INF)
    gmax = jnp.max(lg, axis=-1, keepdims=True)
    lane_f = lane.astype(F32)
    g_idx = jnp.min(jnp.where(lg == gmax, lane_f, float(N_GROUPS)), axis=-1, keepdims=True)
    p_top = 1.0 / jnp.sum(jnp.exp(lg - gmax), axis=-1, keepdims=True)
    lo = N_GROUPS + EXP_PER_GROUP * g_idx
    in_group = (lane_f >= lo) & (lane_f < lo + EXP_PER_GROUP)
    le = jnp.where(in_group, logits, NEG_INF)
    v1 = jnp.max(le, axis=-1, keepdims=True)
    i1 = jnp.min(jnp.where(le == v1, lane_f, float(LANES)), axis=-1, keepdims=True)
    le2 = jnp.where(lane_f == i1, NEG_INF, le)
    v2 = jnp.max(le2, axis=-1, keepdims=True)
    i2 = jnp.min(jnp.where(le2 == v2, lane_f, float(LANES)), axis=-1, keepdims=True)
    e2 = jnp.exp(v2 - v1)
    w1 = p_top / (1.0 + e2)
    w2 = p_top * e2 / (1.0 + e2)
    comb = jnp.where(lane_f == i1, w1, 0.0) + jnp.where(lane_f == i2, w2, 0.0)
    return comb[:, N_GROUPS:N_GROUPS + N_EXPERTS]


def _out_proj_body(ya_ref, yb_ref, yc_ref, x_ref, w_ref, g_ref, b_ref, wr_ref, br_ref,
                   x1_ref, comb_ref, *, alpha, wa, wb, precise):
    mixed = (_mm(ya_ref[...], w_ref[0:wa, :], precise)
             + _mm(yb_ref[...], w_ref[wa:wa + wb, :], precise)
             + _mm(yc_ref[...], w_ref[wa + wb:, :], precise))
    x1 = _layer_norm(alpha * x_ref[...] + mixed, g_ref[...], b_ref[...])
    x1_ref[...] = x1
    nl = br_ref.shape[1]
    x1h, x1l = _split_hi_lo(x1)
    hh = _dot(x1h.astype(BF16), wr_ref[...])
    hl = _dot(x1l.astype(BF16), wr_ref[...])
    logits = hh[:, :nl] + hh[:, nl:] + hl[:, :nl] + hl[:, nl:] + br_ref[...]
    comb_ref[...] = _route(logits)


def _out_proj(ya, yb, yc, x, w, g, b, wr, br, alpha):
    t, d = x.shape
    wa, wb, wc = ya.shape[1], yb.shape[1], yc.shape[1]
    tm = _row_tile(t, PROJ_ROWS)
    rowspec = lambda w: pl.BlockSpec((tm, w), lambda i: (i, 0))
    const = lambda shape: pl.BlockSpec(shape, lambda i: (0,) * len(shape))
    return pl.pallas_call(
        functools.partial(_out_proj_body, alpha=alpha, wa=wa, wb=wb, precise=w.dtype == F32),
        grid=(t // tm,),
        in_specs=[rowspec(wa), rowspec(wb), rowspec(wc), rowspec(d), const(w.shape),
                  const((1, d)), const((1, d)), const(wr.shape), const(br.shape)],
        out_specs=[rowspec(d), rowspec(N_EXPERTS)],
        out_shape=[jax.ShapeDtypeStruct((t, d), F32), jax.ShapeDtypeStruct((t, N_EXPERTS), F32)],
        compiler_params=_cparams("parallel"),
        name="out_proj",
    )(ya, yb, yc, x, w, g, b, wr, br)


def _moe_body(x_ref, comb_ref, wg_ref, wu_ref, wd_ref, g_ref, b_ref, o_ref, xb_ref, acc_ref,
              *, alpha, per_step):
    s = pl.program_id(1)

    @pl.when(s == 0)
    def _():
        acc_ref[...] = jnp.zeros(acc_ref.shape, F32)
        xb_ref[...] = x_ref[...].astype(xb_ref.dtype)

    prec = HIGHEST if xb_ref.dtype == F32 else None
    xb = xb_ref[...]
    comb = comb_ref[...]
    lane = lax.broadcasted_iota(jnp.int32, comb.shape, 1)
    hids = []
    for i in range(per_step):
        ce = jnp.sum(jnp.where(lane == s * per_step + i, comb, 0.0), axis=-1, keepdims=True)
        hid = _silu(_dot(xb, wg_ref[i], prec)) * _dot(xb, wu_ref[i], prec) * ce
        hids.append(hid.astype(xb_ref.dtype))
    acc_ref[...] += _dot(jnp.concatenate(hids, axis=1), wd_ref[...], prec)

    @pl.when(s == pl.num_programs(1) - 1)
    def _():
        o_ref[...] = _layer_norm(alpha * x_ref[...] + acc_ref[...], g_ref[...], b_ref[...])


def _moe(x, comb, wg, wu, wd, g, b, alpha):
    t, d = x.shape
    ne, _, f = wg.shape
    tm = _row_tile(t, MOE_ROWS)
    per_step = EXP_PER_GROUP
    return pl.pallas_call(
        functools.partial(_moe_body, alpha=alpha, per_step=per_step),
        grid=(t // tm, ne // per_step),
        in_specs=[pl.BlockSpec((tm, d), lambda i, s: (i, 0)),
                  pl.BlockSpec((tm, ne), lambda i, s: (i, 0)),
                  pl.BlockSpec((per_step, d, f), lambda i, s: (s, 0, 0)),
                  pl.BlockSpec((per_step, d, f), lambda i, s: (s, 0, 0)),
                  pl.BlockSpec((per_step * f, d), lambda i, s: (s, 0)),
                  pl.BlockSpec((1, d), lambda i, s: (0, 0)),
                  pl.BlockSpec((1, d), lambda i, s: (0, 0))],
        out_specs=pl.BlockSpec((tm, d), lambda i, s: (i, 0)),
        out_shape=jax.ShapeDtypeStruct((t, d), F32),
        scratch_shapes=[pltpu.VMEM((tm, d), wg.dtype), pltpu.VMEM((tm, d), F32)],
        compiler_params=_cparams("parallel", "arbitrary"),
        name="moe",
    )(x, comb, wg, wu, wd, g, b)


def _block_diag(w):
    n, c, _ = w.shape
    eye = jnp.eye(n, dtype=w.dtype)
    return jnp.einsum("ncd,nm->ncmd", w, eye).reshape(n * c, n * c)


def kernel(x_prompt, x_sample, cache_k, cache_v, state_conv, state_rglru, state_hgrn, page_table,
           emb_ln_g, emb_ln_b, w_in, conv_w, conv_b, rg_wr, rg_br, rg_wi, rg_bi, rg_lambda,
           hg_lower, hg_norm_g, dl_lq1, dl_lk1, dl_lq2, dl_lk2, dl_norm_g, w_out,
           ln1_g, ln1_b, ln2_g, ln2_b, moe_wg, moe_bg, moe_we, moe_be,
           ex_w_gate, ex_w_up, ex_w_down):
    depth = w_in.shape[0]
    bp, seq, d = x_prompt.shape
    bs, dseq, _ = x_sample.shape
    assert dseq == 1
    wa = conv_w.shape[-1]
    hk = hg_lower.shape[-1]
    wc = C_HEADS * dl_norm_g.shape[-1]
    splits = (2 * wa, 4 * hk, wc, wc, wc)
    alpha = (2 * depth) ** 0.25
    row = lambda v: v.reshape(1, -1)

    xp = x_prompt.reshape(bp * seq, d)
    xs = x_sample.reshape(bs, d)
    pool, page = cache_k.shape[1], cache_k.shape[2]
    ck = cache_k.reshape(depth, pool, page * C_HEADS, wc // C_HEADS)
    cvv = cache_v.reshape(depth, pool, page * C_HEADS, wc // C_HEADS)
    zeros_conv = jnp.zeros((bp, CONV_W - 1, wa), F32)
    zeros_h = jnp.zeros((bp, wa), F32)
    zeros_st = jnp.zeros((bp, hk, hk), F32)

    outs = {n: [] for n in ("conv_p", "conv_s", "lru_p", "lru_s", "hg_p", "hg_s",
                            "k_p", "k_s", "v_p", "v_s")}
    for l in range(depth):
        lam_init = 0.8 - 0.6 * math.exp(-0.3 * l)
        w_in_b = w_in[l].astype(BF16)
        w_out_b = w_out[l].astype(BF16)
        gate_w32 = jnp.concatenate([_block_diag(rg_wr[l]), _block_diag(rg_wi[l])], axis=1)
        gate_w = gate_w32.astype(BF16)
        gate_b = jnp.concatenate([rg_br[l], rg_bi[l]]).reshape(1, -1)
        hg_ng = row(jnp.tile(hg_norm_g[l], B_HEADS))
        dl_ng = row(dl_norm_g[l])
        lq1, lk1, lq2, lk2 = row(dl_lq1[l]), row(dl_lk1[l]), row(dl_lq2[l]), row(dl_lk2[l])
        pad = jnp.zeros((d, LANES - N_GROUPS - N_EXPERTS), F32)
        wr = jnp.concatenate([moe_wg[l], moe_we[l], pad], axis=1)
        wr = jnp.concatenate(_split_hi_lo(wr), axis=1).astype(BF16)
        br = jnp.concatenate([moe_bg[l], moe_be[l], jnp.zeros((LANES - N_GROUPS - N_EXPERTS,), F32)]).reshape(1, -1)
        wg_b, wu_b = ex_w_gate[l].astype(BF16), ex_w_up[l].astype(BF16)
        wd_b = ex_w_down[l].astype(BF16).reshape(-1, d)
        a_params =lambda gw: (conv_w[l], row(conv_b[l]), gw, gate_b, row(rg_lambda[l]))

        def tail(ya, yb, yc, x, precise):
            wo, eg, eu, ed = ((w_out[l], ex_w_gate[l], ex_w_up[l], ex_w_down[l].reshape(-1, d))
                              if precise else (w_out_b, wg_b, wu_b, wd_b))
            x1, comb = _out_proj(ya, yb, yc, x, wo, row(ln1_g[l]), row(ln1_b[l]), wr, br, alpha)
            return _moe(x1, comb, eg, eu, ed, row(ln2_g[l]), row(ln2_b[l]), alpha)

        res = _proj_in(xp, row(emb_ln_g), row(emb_ln_b), w_in_b, splits[:4], apply_ln=(l == 0),
                       seq=seq, heads=C_HEADS)
        if l == 0:
            xp = res[0]
            res = res[1:]
        za, zb, qc, kc, vct, k_rows, v_rows = res
        ya, conv_new, h_new = _mixer_a_prompt(za.reshape(bp, seq, 2 * wa), zeros_conv, zeros_h,
                                              *a_params(gate_w))
        yb, st_new = _mixer_b_prompt(zb.reshape(bp, seq, 4 * hk), zeros_st, hg_lower, hg_ng, l)
        yc = _attn_prompt(qc.reshape(bp, seq, wc), kc.reshape(bp, seq, wc), vct,
                          lq1, lk1, lq2, lk2, dl_ng, lam_init)
        xp = tail(ya.reshape(bp * seq, wa), yb.reshape(bp * seq, hk), yc.reshape(bp * seq, wc), xp, False)
        outs["conv_p"].append(conv_new)
        outs["lru_p"].append(h_new.reshape(bp, wa))
        outs["hg_p"].append(_state_from_block_diag(st_new, B_HEADS))
        outs["k_p"].append(k_rows.reshape(bp, seq, C_HEADS, wc // C_HEADS))
        outs["v_p"].append(v_rows.reshape(bp, seq, C_HEADS, wc // C_HEADS))

        res = _proj_in_step(xs, row(emb_ln_g), row(emb_ln_b), w_in[l], splits, apply_ln=(l == 0))
        if l == 0:
            xs = res[0]
            res = res[1:]
        za, zb, qc, kc, vc = res
        ya, conv_new, h_new = _mixer_a_step(za, state_conv[l], state_rglru[l], *a_params(gate_w32))
        yb, st_new = _mixer_b_step(zb, _state_to_block_diag(state_hgrn[l]), hg_lower, hg_ng, l)
        yc = _attn_paged(qc, kc, vc, ck, cvv, page_table, l, lq1, lk1, lq2, lk2, dl_ng, lam_init)
        xs = tail(ya, yb, yc, xs, True)
        outs["conv_s"].append(conv_new)
        outs["lru_s"].append(h_new)
        outs["hg_s"].append(_state_from_block_diag(st_new, B_HEADS))
        outs["k_s"].append(kc.reshape(bs, 1, C_HEADS, wc // C_HEADS))
        outs["v_s"].append(vc.reshape(bs, 1, C_HEADS, wc // C_HEADS))

    stack = lambda n: jnp.stack(outs[n])
    return (xp.reshape(bp, seq, d), xs.reshape(bs, 1, d),
            stack("conv_p"), stack("conv_s"), stack("lru_p"), stack("lru_s"),
            stack("hg_p"), stack("hg_s"), stack("k_p"), stack("k_s"), stack("v_p"), stack("v_s"))
```
